```python
import math
import jax, jax.numpy as jnp
from jax import lax
import numpy as np

D_MODEL = 1024
BATCH = 8
SEQ = 4096
DEPTH = 2

A_CH = 512
CONV_W = 31
B_HEADS = 8
B_HEAD_DIM = 64
B_CH = B_HEADS * B_HEAD_DIM
SGU_CHUNK = 128
MIX_IN = 2 * A_CH + 2 * B_CH
MIX_OUT = A_CH + B_CH
MOBA_HEADS = 16
MOBA_HEAD_DIM = 64
MOBA_BLOCK = 256
MOBA_TOPK = 3
MOBA_Q_BLOCK = 128
REL_BUCKETS = 32
REL_MAX_DIST = 2048
PEER_HEADS = 8
PEER_NKEYS = 128
PEER_N_EXPERTS = PEER_NKEYS * PEER_NKEYS
PEER_DKEY = 256
PEER_TOPK = 16
PEER_TOKEN_BLOCK = 128
EPS = 1e-6
NEG = -1e30

kernel_name = "hybrid_conv_sgu_moba_peer"


def rms_norm(x, g):
    xf = x.astype(jnp.float32)
    y = xf * lax.rsqrt(jnp.mean(xf * xf, axis=-1, keepdims=True) + EPS)
    return (y * g.astype(jnp.float32)).astype(x.dtype)


def layer_norm(x, g, b):
    xf = x.astype(jnp.float32)
    mu = jnp.mean(xf, axis=-1, keepdims=True)
    var = jnp.mean(jnp.square(xf - mu), axis=-1, keepdims=True)
    y = (xf - mu) * lax.rsqrt(var + EPS)
    return (y * g.astype(jnp.float32) + b.astype(jnp.float32)).astype(x.dtype)


def t5_bucket(dist):
    n = jnp.maximum(dist, 0)
    max_exact = REL_BUCKETS // 2
    nf = jnp.maximum(n, 1).astype(jnp.float32)
    log_b = max_exact + (jnp.log(nf / max_exact) / math.log(REL_MAX_DIST / max_exact)
                         * (REL_BUCKETS - max_exact)).astype(jnp.int32)
    return jnp.where(n < max_exact, n, jnp.minimum(log_b, REL_BUCKETS - 1))


def conv_sgu_mixer(h, w_in, conv_w, conv_b, conv_ln_g, conv_ln_b,
                   sgu_ln_g, sgu_ln_b, sgu_w, sgu_b, w_out):
    B, S, _ = h.shape
    z = h @ w_in
    a_val, a_gate, b_u, b_v = jnp.split(z, [A_CH, 2 * A_CH, 2 * A_CH + B_CH], axis=-1)
    a = a_val * jax.nn.sigmoid(a_gate)
    a = lax.conv_general_dilated(
        a, conv_w.reshape(CONV_W, 1, A_CH), window_strides=(1,),
        padding=[(CONV_W - 1, 0)], dimension_numbers=('NWC', 'WIO', 'NWC'),
        feature_group_count=A_CH) + conv_b
    a = jax.nn.silu(layer_norm(a, conv_ln_g, conv_ln_b))
    u = jax.nn.gelu(b_u)
    v = layer_norm(jax.nn.gelu(b_v), sgu_ln_g, sgu_ln_b)
    n_chunk = S // SGU_CHUNK
    v = v.reshape(B, n_chunk, SGU_CHUNK, B_HEADS, B_HEAD_DIM)
    mask = jnp.tril(jnp.ones((SGU_CHUNK, SGU_CHUNK), dtype=sgu_w.dtype))
    s = jnp.einsum('hts,bnshc->bnthc', sgu_w * mask, v) + sgu_b.T[None, None, :, :, None]
    bo = u * s.reshape(B, S, B_CH)
    return jnp.concatenate([a, bo], axis=-1) @ w_out


def moba_attention(q, k, v, rel_bias):
    B, S, H, hd = q.shape
    n_blk = -(-S // MOBA_BLOCK)
    s_pad = n_blk * MOBA_BLOCK
    pad = ((0, 0), (0, s_pad - S), (0, 0), (0, 0))
    kb = jnp.pad(k, pad).reshape(B, n_blk, MOBA_BLOCK, H, hd).transpose(0, 3, 1, 2, 4)
    vb = jnp.pad(v, pad).reshape(B, n_blk, MOBA_BLOCK, H, hd).transpose(0, 3, 1, 2, 4)
    k_mean = jnp.mean(kb.astype(jnp.float32), axis=3)
    gate = jnp.einsum('bshd,bhnd->bshn', q.astype(jnp.float32), k_mean)
    own_blk = jnp.arange(S) // MOBA_BLOCK
    past = jnp.arange(n_blk)[None, :] < own_blk[:, None]
    gate = jnp.where(past[None, :, None, :], gate, -jnp.inf)
    k_sel = min(MOBA_TOPK, n_blk)
    _, sel = lax.top_k(gate, k_sel)
    nqc = S // MOBA_Q_BLOCK
    qc_all = q.reshape(B * nqc, MOBA_Q_BLOCK, H, hd)
    sel_all = sel.reshape(B * nqc, MOBA_Q_BLOCK, H, k_sel)
    idx = jnp.arange(B * nqc)
    b_all = idx // nqc
    c_all = idx % nqc
    h_idx = jnp.arange(H)[None, :, None]
    offs = jnp.arange(MOBA_BLOCK)
    scale = hd ** -0.5
    n_past = k_sel * MOBA_BLOCK

    def step(args):
        qc, selc, b, c = args
        t = c * MOBA_Q_BLOCK + jnp.arange(MOBA_Q_BLOCK)
        own = (c * MOBA_Q_BLOCK) // MOBA_BLOCK
        kp = kb[b, h_idx, selc]
        vp = vb[b, h_idx, selc]
        ko = kb[b, :, own]
        vo = vb[b, :, own]
        dist_p = t[:, None, None, None] - (selc[..., None] * MOBA_BLOCK + offs)
        bias_p = rel_bias[t5_bucket(dist_p), h_idx[..., None]].astype(jnp.float32)
        lp = jnp.einsum('qhd,qhjkd->qhjk', qc, kp).astype(jnp.float32) * scale + bias_p
        lp = jnp.where((selc < own)[..., None], lp, NEG)
        dist_o = t[:, None] - (own * MOBA_BLOCK + offs)
        bias_o = rel_bias[t5_bucket(dist_o)].transpose(0, 2, 1).astype(jnp.float32)
        lo = jnp.einsum('qhd,hkd->qhk', qc, ko).astype(jnp.float32) * scale + bias_o
        lo = jnp.where((dist_o >= 0)[:, None, :], lo, NEG)
        logits = jnp.concatenate([lp.reshape(MOBA_Q_BLOCK, H, n_past), lo], axis=-1)
        p = jax.nn.softmax(logits, axis=-1).astype(vb.dtype)
        pp = p[..., :n_past].reshape(MOBA_Q_BLOCK, H, k_sel, MOBA_BLOCK)
        po = p[..., n_past:]
        return (jnp.einsum('qhjk,qhjkd->qhd', pp, vp)
                + jnp.einsum('qhk,hkd->qhd', po, vo))

    out = lax.map(step, (qc_all, sel_all, b_all, c_all))
    return out.reshape(B, S, H * hd)


def moba_mixer(h, w_qkv, w_o, rel_bias):
    B, S, _ = h.shape
    qkv = (h @ w_qkv).reshape(B, S, 3, MOBA_HEADS, MOBA_HEAD_DIM)
    o = moba_attention(qkv[:, :, 0], qkv[:, :, 1], qkv[:, :, 2], rel_bias)
    return o @ w_o


def peer(h, w_q, subkeys, u_tab, v_tab):
    B, S, D = h.shape
    T = B * S
    xt = h.reshape(T, D)
    q = (xt @ w_q).reshape(T, PEER_HEADS, 2, PEER_DKEY // 2)
    s = jnp.einsum('thpd,hpkd->thpk', q, subkeys).astype(jnp.float32)
    s1, i1 = lax.top_k(s[:, :, 0], PEER_TOPK)
    s2, i2 = lax.top_k(s[:, :, 1], PEER_TOPK)
    cand = (s1[..., :, None] + s2[..., None, :]).reshape(T, PEER_HEADS, PEER_TOPK * PEER_TOPK)
    sc, ci = lax.top_k(cand, PEER_TOPK)
    ei = jnp.take_along_axis(i1, ci // PEER_TOPK, axis=-1)
    ej = jnp.take_along_axis(i2, ci % PEER_TOPK, axis=-1)
    expert = ei * PEER_NKEYS + ej
    g = jax.nn.softmax(sc, axis=-1).astype(h.dtype)
    nc = T // PEER_TOKEN_BLOCK

    def step(args):
        xc, ec, gc = args
        uc = u_tab[ec]
        vc = v_tab[ec]
        act = jax.nn.gelu(jnp.einsum('td,thkd->thk', xc, uc))
        return jnp.einsum('thk,thkd->td', gc * act, vc)

    y = lax.map(step, (xt.reshape(nc, PEER_TOKEN_BLOCK, D),
                       expert.reshape(nc, PEER_TOKEN_BLOCK, PEER_HEADS, PEER_TOPK),
                       g.reshape(nc, PEER_TOKEN_BLOCK, PEER_HEADS, PEER_TOPK)))
    return y.reshape(B, S, D)


def setup_inputs(seed: int = 0) -> dict:
    key = jax.random.key(seed)
    ks = jax.random.split(key, 24)
    n_even = (DEPTH + 1) // 2
    n_odd = DEPTH // 2
    f32 = jnp.float32

    def nrm(k, shape, scale):
        return jax.random.normal(k, shape, f32) * scale

    attn_w = MOBA_HEADS * MOBA_HEAD_DIM
    return {
        "x": nrm(ks[0], (BATCH, SEQ, D_MODEL), 1.0),
        "ev_w_in": nrm(ks[1], (n_even, D_MODEL, MIX_IN), D_MODEL ** -0.5),
        "ev_conv_w": nrm(ks[2], (n_even, CONV_W, A_CH), CONV_W ** -0.5),
        "ev_conv_b": nrm(ks[3], (n_even, A_CH), 0.02),
        "ev_conv_ln_g": 1.0 + nrm(ks[4], (n_even, A_CH), 0.02),
        "ev_conv_ln_b": nrm(ks[5], (n_even, A_CH), 0.02),
        "ev_sgu_ln_g": 1.0 + nrm(ks[6], (n_even, B_CH), 0.02),
        "ev_sgu_ln_b": nrm(ks[7], (n_even, B_CH), 0.02),
        "ev_sgu_w": nrm(ks[8], (n_even, B_HEADS, SGU_CHUNK, SGU_CHUNK), SGU_CHUNK ** -0.5),
        "ev_sgu_b": 1.0 + nrm(ks[9], (n_even, B_HEADS, SGU_CHUNK), 0.02),
        "ev_w_out": nrm(ks[10], (n_even, MIX_OUT, D_MODEL), MIX_OUT ** -0.5),
        "od_w_qkv": nrm(ks[11], (n_odd, D_MODEL, 3 * attn_w), D_MODEL ** -0.5),
        "od_w_o": nrm(ks[12], (n_odd, attn_w, D_MODEL), attn_w ** -0.5),
        "rel_bias": nrm(ks[13], (REL_BUCKETS, MOBA_HEADS), 0.5),
        "peer_w_q": nrm(ks[14], (DEPTH, D_MODEL, PEER_HEADS * PEER_DKEY), D_MODEL ** -0.5),
        "peer_subkeys": nrm(ks[15], (DEPTH, PEER_HEADS, 2, PEER_NKEYS, PEER_DKEY // 2),
                            (PEER_DKEY // 2) ** -0.5),
        "peer_u": nrm(ks[16], (DEPTH, PEER_N_EXPERTS, D_MODEL), D_MODEL ** -0.5),
        "peer_v": nrm(ks[17], (DEPTH, PEER_N_EXPERTS, D_MODEL), PEER_HEADS ** -0.5),
        "norm_mix_g": 1.0 + nrm(ks[18], (DEPTH, D_MODEL), 0.02),
        "norm_ffn_g": 1.0 + nrm(ks[19], (DEPTH, D_MODEL), 0.02),
        "norm_final_g": 1.0 + nrm(ks[20], (D_MODEL,), 0.02),
    }


def reference(x, ev_w_in, ev_conv_w, ev_conv_b, ev_conv_ln_g, ev_conv_ln_b,
              ev_sgu_ln_g, ev_sgu_ln_b, ev_sgu_w, ev_sgu_b, ev_w_out,
              od_w_qkv, od_w_o, rel_bias, peer_w_q, peer_subkeys, peer_u, peer_v,
              norm_mix_g, norm_ffn_g, norm_final_g):
    for l in range(DEPTH):
        h = rms_norm(x, norm_mix_g[l])
        i = l // 2
        if l % 2 == 0:
            mix = conv_sgu_mixer(h, ev_w_in[i], ev_conv_w[i], ev_conv_b[i],
                                 ev_conv_ln_g[i], ev_conv_ln_b[i], ev_sgu_ln_g[i],
                                 ev_sgu_ln_b[i], ev_sgu_w[i], ev_sgu_b[i], ev_w_out[i])
        else:
            mix = moba_mixer(h, od_w_qkv[i], od_w_o[i], rel_bias)
        x = x + mix
        h = rms_norm(x, norm_ffn_g[l])
        x = x + peer(h, peer_w_q[l], peer_subkeys[l], peer_u[l], peer_v[l])
    return rms_norm(x, norm_final_g)
```

```python
import functools
import math

import jax
import jax.numpy as jnp
from jax import lax
from jax.experimental import pallas as pl
from jax.experimental.pallas import tpu as pltpu

D_MODEL = 1024
A_CH = 512
CONV_W = 31
B_HEADS = 8
B_HEAD_DIM = 64
B_CH = B_HEADS * B_HEAD_DIM
SGU_CHUNK = 128
MIX_IN = 2 * A_CH + 2 * B_CH
MOBA_HEADS = 16
MOBA_HEAD_DIM = 64
MOBA_BLOCK = 256
MOBA_TOPK = 3
REL_BUCKETS = 32
REL_MAX_DIST = 2048
PEER_HEADS = 8
PEER_NKEYS = 128
PEER_N_EXPERTS = PEER_NKEYS * PEER_NKEYS
PEER_DKEY = 256
PEER_TOPK = 16
EPS = 1e-6
NEG = -1e30

LANES = 128
MXU_DTYPE = jnp.bfloat16
VMEM_LIMIT = 56 * 1024 * 1024

MIX_TS = 512
CONV_HALO = 32
PEER_TB = 512
PEER_EC = 1024
MM_TM = 512
ATT_TQ = MOBA_BLOCK
BIAS_TILES = 8


def _rms(x, g):
    return x * lax.rsqrt(jnp.mean(x * x, axis=-1, keepdims=True) + EPS) * g


def _layer_norm(x, g, b):
    mu = jnp.mean(x, axis=-1, keepdims=True)
    var = jnp.mean(jnp.square(x - mu), axis=-1, keepdims=True)
    return (x - mu) * lax.rsqrt(var + EPS) * g + b


def _gelu(x):
    cdf = 0.5 * (1.0 + jnp.tanh(math.sqrt(2.0 / math.pi) * (x + 0.044715 * (x * x * x))))
    return x * cdf


def _mm(a, b):
    return jnp.dot(a.astype(MXU_DTYPE), b.astype(MXU_DTYPE), preferred_element_type=jnp.float32)


def _mm_nt(a, b):
    return lax.dot_general(a.astype(MXU_DTYPE), b.astype(MXU_DTYPE), (((1,), (1,)), ((), ())),
                           preferred_element_type=jnp.float32)


def _mixer_kernel(x_ref, g_ref, win_ref, cw_ref, cb_ref, clg_ref, clb_ref, slg_ref, slb_ref,
                  sw_ref, sb_ref, wout_ref, o_ref, abuf_ref, s_ref):
    ts = x_ref.shape[1]
    x = x_ref[0]
    h = _rms(x, g_ref[...])
    z = _mm(h, win_ref[...])
    a = z[:, :A_CH] * jax.nn.sigmoid(z[:, A_CH:2 * A_CH])

    @pl.when(pl.program_id(1) == 0)
    def _():
        abuf_ref[0:CONV_HALO, :] = jnp.zeros((CONV_HALO, A_CH), jnp.float32)

    @pl.when(pl.program_id(1) > 0)
    def _():
        abuf_ref[0:CONV_HALO, :] = abuf_ref[ts:ts + CONV_HALO, :]

    abuf_ref[CONV_HALO:CONV_HALO + ts, :] = a
    acc = jnp.broadcast_to(cb_ref[...], (ts, A_CH))
    for w in range(CONV_W):
        off = CONV_HALO - (CONV_W - 1) + w
        acc = acc + cw_ref[w:w + 1, :] * abuf_ref[off:off + ts, :]
    a_n = _layer_norm(acc, clg_ref[...], clb_ref[...])
    a_out = a_n * jax.nn.sigmoid(a_n)

    u = _gelu(z[:, 2 * A_CH:2 * A_CH + B_CH])
    v = _layer_norm(_gelu(z[:, 2 * A_CH + B_CH:]), slg_ref[...], slb_ref[...])
    row = lax.broadcasted_iota(jnp.int32, (SGU_CHUNK, SGU_CHUNK), 0)
    col = lax.broadcasted_iota(jnp.int32, (SGU_CHUNK, SGU_CHUNK), 1)
    low_head = lax.broadcasted_iota(jnp.int32, (SGU_CHUNK, LANES), 1) < B_HEAD_DIM
    wm = [jnp.where(col <= row, sw_ref[hh], 0.0).astype(MXU_DTYPE) for hh in range(B_HEADS)]
    for c in range(ts // SGU_CHUNK):
        for pr in range(B_HEADS // 2):
            vp = v[c * SGU_CHUNK:(c + 1) * SGU_CHUNK, pr * LANES:(pr + 1) * LANES].astype(MXU_DTYPE)
            s0 = jnp.dot(wm[2 * pr], vp, preferred_element_type=jnp.float32)
            s1 = jnp.dot(wm[2 * pr + 1], vp, preferred_element_type=jnp.float32)
            s_ref[c * SGU_CHUNK:(c + 1) * SGU_CHUNK, pr * LANES:(pr + 1) * LANES] = (
                jnp.where(low_head, s0, s1) + sb_ref[:, pr * LANES:(pr + 1) * LANES])
    bo = u * s_ref[...]
    out = _mm(a_out, wout_ref[0:A_CH, :]) + _mm(bo, wout_ref[A_CH:, :])
    o_ref[0] = x + out


def _mixer(x, g, w_in, conv_w, conv_b, cln_g, cln_b, sln_g, sln_b, sgu_w, sgu_b, w_out):
    B, S, D = x.shape
    ts = min(MIX_TS, S)
    assert S % ts == 0 and ts % SGU_CHUNK == 0
    row2 = lambda a: a.reshape(1, -1)
    sb_exp = jnp.repeat(sgu_b.T, B_HEAD_DIM, axis=1)
    const = lambda shape: pl.BlockSpec(shape, lambda b, s: (0,) * len(shape))
    return pl.pallas_call(
        _mixer_kernel,
        grid=(B, S // ts),
        in_specs=[
            pl.BlockSpec((1, ts, D), lambda b, s: (b, s, 0)),
            const((1, D)), const((D, MIX_IN)), const((CONV_W, A_CH)), const((1, A_CH)),
            const((1, A_CH)), const((1, A_CH)), const((1, B_CH)), const((1, B_CH)),
            const((B_HEADS, SGU_CHUNK, SGU_CHUNK)), const((SGU_CHUNK, B_CH)), const((A_CH + B_CH, D)),
        ],
        out_specs=pl.BlockSpec((1, ts, D), lambda b, s: (b, s, 0)),
        out_shape=jax.ShapeDtypeStruct((B, S, D), jnp.float32),
        scratch_shapes=[pltpu.VMEM((ts + CONV_HALO, A_CH), jnp.float32),
                        pltpu.VMEM((ts, B_CH), jnp.float32)],
        compiler_params=pltpu.CompilerParams(dimension_semantics=("arbitrary", "arbitrary"),
                                             vmem_limit_bytes=VMEM_LIMIT),
        name="mixer0",
    )(x, row2(g), w_in.astype(MXU_DTYPE), conv_w, row2(conv_b), row2(cln_g), row2(cln_b),
      row2(sln_g), row2(sln_b), sgu_w, sb_exp, w_out.astype(MXU_DTYPE))


def _candidate_pairs():
    return [(a, b) for a in range(PEER_TOPK) for b in range(PEER_TOPK) if (a + 1) * (b + 1) <= PEER_TOPK]


def _top16(s, iota_f, vals_ref, p, hh):
    work = s
    rank = jnp.full(s.shape, float(PEER_TOPK), jnp.float32)
    for r in range(PEER_TOPK):
        m = jnp.max(work, axis=0, keepdims=True)
        first = jnp.min(jnp.where(work == m, iota_f, float(PEER_NKEYS)), axis=0, keepdims=True)
        sel = iota_f == first
        rank = jnp.where(sel, float(r), rank)
        work = jnp.where(sel, -jnp.inf, work)
        vals_ref[p, r, pl.ds(hh, 1), :] = m
    return rank


def _peer_kernel(x_ref, g_ref, wq_ref, sk_ref, u_ref, vt_ref, gf_ref, o_ref,
                 ht_ref, yt_ref, q_ref, rank1_ref, rank2_ref, e1_ref, e2_ref, n_ref,
                 vals_ref, na_ref, p_ref, *, final_norm):
    c = pl.program_id(1)
    tb = x_ref.shape[0]
    ec = u_ref.shape[0]

    @pl.when(c == 0)
    def _route():
        h = _rms(x_ref[...], g_ref[...])
        ht = h.T.astype(MXU_DTYPE)
        ht_ref[...] = ht
        q_ref[...] = jnp.dot(wq_ref[...], ht, preferred_element_type=jnp.float32)
        yt_ref[...] = jnp.zeros_like(yt_ref)
        iota_f = lax.broadcasted_iota(jnp.int32, (PEER_NKEYS, tb), 0).astype(jnp.float32)

        def head_scores(hh, carry):
            for p, rank_ref, e_ref in ((0, rank1_ref, e1_ref), (1, rank2_ref, e2_ref)):
                idx = hh * 2 + p
                start = pl.multiple_of(idx * PEER_NKEYS, PEER_NKEYS)
                qhp = q_ref[pl.ds(start, PEER_NKEYS), :]
                s = jnp.dot(sk_ref[idx], qhp.astype(MXU_DTYPE), preferred_element_type=jnp.float32)
                rank_ref[hh] = _top16(s, iota_f, vals_ref, p, hh)
                e_ref[hh] = jnp.exp(s - jnp.max(s, axis=0, keepdims=True))
            return carry

        lax.fori_loop(0, PEER_HEADS, head_scores, 0)

        pairs = _candidate_pairs()
        v1 = [vals_ref[0, a] for a in range(PEER_TOPK)]
        v2 = [vals_ref[1, b] for b in range(PEER_TOPK)]
        cand = {ab: v1[ab[0]] + v2[ab[1]] for ab in pairs}
        pos = {ab: jnp.full((PEER_HEADS, tb), float((ab[0] + 1) * (ab[1] + 1) - 1), jnp.float32) for ab in pairs}
        for i, x_ab in enumerate(pairs):
            for y_ab in pairs[i + 1:]:
                (a, b), (a2, b2) = x_ab, y_ab
                if (a2 >= a and b2 >= b) or (a2 <= a and b2 <= b):
                    continue
                x_first = cand[x_ab] >= cand[y_ab]
                pos[y_ab] = pos[y_ab] + jnp.where(x_first, 1.0, 0.0)
                pos[x_ab] = pos[x_ab] + jnp.where(x_first, 0.0, 1.0)
        c00 = cand[(0, 0)]
        z = jnp.zeros((PEER_HEADS, tb), jnp.float32)
        n_a = [jnp.zeros((PEER_HEADS, tb), jnp.float32) for _ in range(PEER_TOPK)]
        for (a, b) in pairs:
            chosen = pos[(a, b)] < float(PEER_TOPK)
            n_a[a] = n_a[a] + jnp.where(chosen, 1.0, 0.0)
            z = z + jnp.where(chosen, jnp.exp(cand[(a, b)] - c00), 0.0)
        inv_z = 1.0 / z
        for a in range(PEER_TOPK):
            na_ref[a] = n_a[a]
        na_ref[PEER_TOPK] = inv_z

        def head_rows(hh, carry):
            rank1 = rank1_ref[hh]
            n_i = jnp.zeros((PEER_NKEYS, tb), jnp.float32)
            for a in range(PEER_TOPK):
                n_i = jnp.where(rank1 == float(a), na_ref[a, pl.ds(hh, 1), :], n_i)
            n_ref[hh] = n_i
            e1_ref[hh] = e1_ref[hh] * na_ref[PEER_TOPK, pl.ds(hh, 1), :]
            return carry

        lax.fori_loop(0, PEER_HEADS, head_rows, 0)

    act_in = jnp.dot(u_ref[...], ht_ref[...], preferred_element_type=jnp.float32)
    for g in range(ec // PEER_NKEYS):
        i = c * (ec // PEER_NKEYS) + g
        act = _gelu(act_in[g * PEER_NKEYS:(g + 1) * PEER_NKEYS, :])
        wgt = jnp.zeros((PEER_NKEYS, tb), jnp.float32)
        for hh in range(PEER_HEADS):
            n_row = n_ref[hh, pl.ds(i, 1), :]
            e1_row = e1_ref[hh, pl.ds(i, 1), :]
            wgt = wgt + jnp.where(rank2_ref[hh] < n_row, e2_ref[hh], 0.0) * e1_row
        p_ref[g * PEER_NKEYS:(g + 1) * PEER_NKEYS, :] = (wgt * act).astype(MXU_DTYPE)
    yt_ref[...] += jnp.dot(vt_ref[...], p_ref[...], preferred_element_type=jnp.float32)

    @pl.when(c == pl.num_programs(1) - 1)
    def _finish():
        out = x_ref[...] + yt_ref[...].T
        if final_norm:
            out = _rms(out, gf_ref[...])
        o_ref[...] = out


def _peer(x, g, w_q, subkeys, u_tab, v_tab, g_final, final_norm):
    T, D = x.shape
    tb = min(PEER_TB, T)
    ec = PEER_EC
    assert T % tb == 0 and PEER_N_EXPERTS % ec == 0
    n_chunk = PEER_N_EXPERTS // ec
    hq = PEER_HEADS * PEER_DKEY
    wq_t = w_q.T.astype(MXU_DTYPE)
    sk = subkeys.reshape(PEER_HEADS * 2, PEER_NKEYS, PEER_DKEY // 2).astype(MXU_DTYPE)
    u_b = u_tab.astype(MXU_DTYPE)
    vt_b = v_tab.T.astype(MXU_DTYPE)
    head_tile = (PEER_HEADS, PEER_NKEYS, tb)
    return pl.pallas_call(
        functools.partial(_peer_kernel, final_norm=final_norm),
        grid=(T // tb, n_chunk),
        in_specs=[
            pl.BlockSpec((tb, D), lambda t, c: (t, 0)),
            pl.BlockSpec((1, D), lambda t, c: (0, 0)),
            pl.BlockSpec((hq, D), lambda t, c: (0, 0)),
            pl.BlockSpec((PEER_HEADS * 2, PEER_NKEYS, PEER_DKEY // 2), lambda t, c: (0, 0, 0)),
            pl.BlockSpec((ec, D), lambda t, c: (c, 0)),
            pl.BlockSpec((D, ec), lambda t, c: (0, c)),
            pl.BlockSpec((1, D), lambda t, c: (0, 0)),
        ],
        out_specs=pl.BlockSpec((tb, D), lambda t, c: (t, 0)),
        out_shape=jax.ShapeDtypeStruct((T, D), jnp.float32),
        scratch_shapes=[
            pltpu.VMEM((D, tb), MXU_DTYPE),
            pltpu.VMEM((D, tb), jnp.float32),
            pltpu.VMEM((hq, tb), jnp.float32),
            pltpu.VMEM(head_tile, jnp.float32),
            pltpu.VMEM(head_tile, jnp.float32),
            pltpu.VMEM(head_tile, jnp.float32),
            pltpu.VMEM(head_tile, jnp.float32),
            pltpu.VMEM(head_tile, jnp.float32),
            pltpu.VMEM((2, PEER_TOPK, PEER_HEADS, tb), jnp.float32),
            pltpu.VMEM((PEER_TOPK + 1, PEER_HEADS, tb), jnp.float32),
            pltpu.VMEM((ec, tb), MXU_DTYPE),
        ],
        compiler_params=pltpu.CompilerParams(dimension_semantics=("arbitrary", "arbitrary"),
                                             vmem_limit_bytes=VMEM_LIMIT),
        name="peer_final" if final_norm else "peer",
    )(x, g.reshape(1, D), wq_t, sk, u_b, vt_b, g_final.reshape(1, D))


def _qkv_kernel(x_ref, g_ref, w_ref, q_ref, k_ref, v_ref):
    h = _rms(x_ref[...], g_ref[...])
    qkv = _mm(h, w_ref[...])
    d = q_ref.shape[1]
    q_ref[...] = qkv[:, :d]
    k_ref[...] = qkv[:, d:2 * d]
    v_ref[...] = qkv[:, 2 * d:]


def _qkv(x, g, w_qkv):
    T, D = x.shape
    tm = min(MM_TM, T)
    aw = MOBA_HEADS * MOBA_HEAD_DIM
    out = jax.ShapeDtypeStruct((T, aw), jnp.float32)
    return pl.pallas_call(
        _qkv_kernel,
        grid=(T // tm,),
        in_specs=[pl.BlockSpec((tm, D), lambda i: (i, 0)),
                  pl.BlockSpec((1, D), lambda i: (0, 0)),
                  pl.BlockSpec((D, 3 * aw), lambda i: (0, 0))],
        out_specs=[pl.BlockSpec((tm, aw), lambda i: (i, 0))] * 3,
        out_shape=[out, out, out],
        compiler_params=pltpu.CompilerParams(dimension_semantics=("arbitrary",),
                                             vmem_limit_bytes=VMEM_LIMIT),
        name="qkv",
    )(x, g.reshape(1, D), w_qkv.astype(MXU_DTYPE))


def _oproj_kernel(x_ref, o_ref, w_ref, y_ref):
    y_ref[...] = x_ref[...] + _mm(o_ref[...], w_ref[...])


def _oproj(x, o, w_o):
    T, D = x.shape
    tm = min(MM_TM, T)
    aw = o.shape[1]
    return pl.pallas_call(
        _oproj_kernel,
        grid=(T // tm,),
        in_specs=[pl.BlockSpec((tm, D), lambda i: (i, 0)),
                  pl.BlockSpec((tm, aw), lambda i: (i, 0)),
                  pl.BlockSpec((aw, D), lambda i: (0, 0))],
        out_specs=pl.BlockSpec((tm, D), lambda i: (i, 0)),
        out_shape=jax.ShapeDtypeStruct((T, D), jnp.float32),
        compiler_params=pltpu.CompilerParams(dimension_semantics=("arbitrary",),
                                             vmem_limit_bytes=VMEM_LIMIT),
        name="oproj",
    )(x, o, w_o.astype(MXU_DTYPE))


def _t5_bucket(dist):
    n = jnp.maximum(dist, 0)
    max_exact = REL_BUCKETS // 2
    nf = jnp.maximum(n, 1).astype(jnp.float32)
    log_b = max_exact + (jnp.log(nf / max_exact) / math.log(REL_MAX_DIST / max_exact)
                         * (REL_BUCKETS - max_exact)).astype(jnp.int32)
    return jnp.where(n < max_exact, n, jnp.minimum(log_b, REL_BUCKETS - 1))


def _bias_kernel(rb_ref, o_ref):
    h = pl.program_id(0)
    dblk = pl.program_id(1)
    key = lax.broadcasted_iota(jnp.int32, (MOBA_BLOCK, MOBA_BLOCK), 0)
    qry = lax.broadcasted_iota(jnp.int32, (MOBA_BLOCK, MOBA_BLOCK), 1)
    bucket = _t5_bucket(dblk * MOBA_BLOCK + qry - key)
    val = jnp.zeros((MOBA_BLOCK, MOBA_BLOCK), jnp.float32)
    for b in range(REL_BUCKETS):
        val = jnp.where(bucket == b, rb_ref[b, h], val)
    o_ref[0, 0] = val


def _bias_tiles(rel_bias):
    last_tile_min_dist = (BIAS_TILES - 1) * MOBA_BLOCK - (MOBA_BLOCK - 1)
    max_exact = REL_BUCKETS // 2
    first_last_bucket = max_exact * (REL_MAX_DIST / max_exact) ** ((REL_BUCKETS - 1 - max_exact) / (REL_BUCKETS - max_exact))
    assert last_tile_min_dist > first_last_bucket + 1
    return pl.pallas_call(
        _bias_kernel,
        grid=(MOBA_HEADS, BIAS_TILES),
        in_specs=[pl.BlockSpec(memory_space=pltpu.SMEM)],
        out_specs=pl.BlockSpec((1, 1, MOBA_BLOCK, MOBA_BLOCK), lambda h, d: (h, d, 0, 0)),
        out_shape=jax.ShapeDtypeStruct((MOBA_HEADS, BIAS_TILES, MOBA_BLOCK, MOBA_BLOCK), jnp.float32),
        compiler_params=pltpu.CompilerParams(dimension_semantics=("arbitrary", "arbitrary")),
        name="t5_bias_tiles",
    )(rel_bias)


def _attn_kernel(q_ref, k_ref, v_ref, bias_ref, o_ref,
                 kb_ref, vt_ref, kmean_ref, sel_ref, m_ref, l_ref, acc_ref):
    qi = pl.program_id(2)
    S = k_ref.shape[1]
    n_blk = S // MOBA_BLOCK
    tq = q_ref.shape[1]

    @pl.when(qi == 0)
    def _prep():
        for n in range(n_blk):
            kblk = k_ref[0, n * MOBA_BLOCK:(n + 1) * MOBA_BLOCK, :]
            kb_ref[n * MOBA_BLOCK:(n + 1) * MOBA_BLOCK, :] = kblk.astype(MXU_DTYPE)
            kmean_ref[n:n + 1, :] = jnp.mean(kblk, axis=0, keepdims=True)
            vt_ref[:, n * MOBA_BLOCK:(n + 1) * MOBA_BLOCK] = (
                v_ref[0, n * MOBA_BLOCK:(n + 1) * MOBA_BLOCK, :].T.astype(MXU_DTYPE))

    own = qi
    q = q_ref[0]
    low_head = lax.broadcasted_iota(jnp.int32, (tq, LANES), 1) < MOBA_HEAD_DIM
    q_heads = [jnp.where(low_head, q, 0.0), jnp.where(low_head, 0.0, q)]
    blk = lax.broadcasted_iota(jnp.int32, (n_blk, tq), 0)
    past = blk < own
    key_i = lax.broadcasted_iota(jnp.int32, (MOBA_BLOCK, tq), 0)
    qry_i = lax.broadcasted_iota(jnp.int32, (MOBA_BLOCK, tq), 1)
    qs = []
    k_own = kb_ref[pl.ds(pl.multiple_of(own * MOBA_BLOCK, MOBA_BLOCK), MOBA_BLOCK), :]
    vt_own = vt_ref[:, pl.ds(pl.multiple_of(own * MOBA_BLOCK, MOBA_BLOCK), MOBA_BLOCK)]
    for e in range(2):
        gate = jnp.where(past, _mm_nt(kmean_ref[...], q_heads[e]), -jnp.inf)
        beaten = jnp.zeros((n_blk, tq), jnp.float32)
        for m in range(n_blk):
            gm = gate[m:m + 1, :]
            beats = (gm > gate) | ((gm == gate) & (blk > m))
            beaten = beaten + jnp.where(beats, 1.0, 0.0)
        sel_ref[e] = jnp.where(past & (beaten < float(MOBA_TOPK)), 1.0, 0.0)

        qe = (q_heads[e] * (MOBA_HEAD_DIM ** -0.5)).astype(MXU_DTYPE)
        qs.append(qe)
        st = _mm_nt(k_own, qe) + bias_ref[e, 0]
        st = jnp.where(key_i <= qry_i, st, NEG)
        m0 = jnp.max(st, axis=0, keepdims=True)
        p = jnp.exp(st - m0)
        m_ref[e] = m0
        l_ref[e] = jnp.sum(p, axis=0, keepdims=True)
        acc_ref[e] = jnp.dot(vt_own, p.astype(MXU_DTYPE), preferred_element_type=jnp.float32)

    def past_block(n, carry):
        start = pl.multiple_of(n * MOBA_BLOCK, MOBA_BLOCK)
        k_n = kb_ref[pl.ds(start, MOBA_BLOCK), :]
        vt_n = vt_ref[:, pl.ds(start, MOBA_BLOCK)]
        tile = jnp.minimum(own - n, BIAS_TILES - 1)
        for e in range(2):
            st = _mm_nt(k_n, qs[e]) + bias_ref[e, tile]
            st = jnp.where(sel_ref[e, pl.ds(n, 1), :] > 0.0, st, NEG)
            m_old = m_ref[e]
            m_new = jnp.maximum(m_old, jnp.max(st, axis=0, keepdims=True))
            alpha = jnp.exp(m_old - m_new)
            p = jnp.exp(st - m_new)
            m_ref[e] = m_new
            l_ref[e] = alpha * l_ref[e] + jnp.sum(p, axis=0, keepdims=True)
            acc_ref[e] = alpha * acc_ref[e] + jnp.dot(vt_n, p.astype(MXU_DTYPE),
                                                      preferred_element_type=jnp.float32)
        return carry

    lax.fori_loop(0, own, past_block, 0)

    o0 = acc_ref[0] / l_ref[0]
    o1 = acc_ref[1] / l_ref[1]
    ot = jnp.concatenate([o0[:MOBA_HEAD_DIM], o1[MOBA_HEAD_DIM:]], axis=0)
    o_ref[0] = ot.T


def _attention(q, k, v, bias_tiles):
    B, S, aw = q.shape
    tq = ATT_TQ
    assert S % MOBA_BLOCK == 0 and tq == MOBA_BLOCK
    n_blk = S // MOBA_BLOCK
    n_pair = aw // LANES
    return pl.pallas_call(
        _attn_kernel,
        grid=(n_pair, B, S // tq),
        in_specs=[
            pl.BlockSpec((1, tq, LANES), lambda hp, b, qi: (b, qi, hp)),
            pl.BlockSpec((1, S, LANES), lambda hp, b, qi: (b, 0, hp)),
            pl.BlockSpec((1, S, LANES), lambda hp, b, qi: (b, 0, hp)),
            pl.BlockSpec((2, BIAS_TILES, MOBA_BLOCK, MOBA_BLOCK), lambda hp, b, qi: (hp, 0, 0, 0)),
        ],
        out_specs=pl.BlockSpec((1, tq, LANES), lambda hp, b, qi: (b, qi, hp)),
        out_shape=jax.ShapeDtypeStruct((B, S, aw), jnp.float32),
        scratch_shapes=[
            pltpu.VMEM((S, LANES), MXU_DTYPE),
            pltpu.VMEM((LANES, S), MXU_DTYPE),
            pltpu.VMEM((n_blk, LANES), jnp.float32),
            pltpu.VMEM((2, n_blk, tq), jnp.float32),
            pltpu.VMEM((2, 1, tq), jnp.float32),
            pltpu.VMEM((2, 1, tq), jnp.float32),
            pltpu.VMEM((2, LANES, tq), jnp.float32),
        ],
        compiler_params=pltpu.CompilerParams(dimension_semantics=("arbitrary", "arbitrary", "arbitrary"),
                                             vmem_limit_bytes=VMEM_LIMIT),
        name="moba_attention",
    )(q, k, v, bias_tiles)


def kernel(x, ev_w_in, ev_conv_w, ev_conv_b, ev_conv_ln_g, ev_conv_ln_b, ev_sgu_ln_g, ev_sgu_ln_b, ev_sgu_w, ev_sgu_b, ev_w_out, od_w_qkv, od_w_o, rel_bias, peer_w_q, peer_subkeys, peer_u, peer_v, norm_mix_g, norm_ffn_g, norm_final_g):
    B, S, D = x.shape
    T = B * S
    x = _mixer(x, norm_mix_g[0], ev_w_in[0], ev_conv_w[0], ev_conv_b[0], ev_conv_ln_g[0], ev_conv_ln_b[0],
               ev_sgu_ln_g[0], ev_sgu_ln_b[0], ev_sgu_w[0], ev_sgu_b[0], ev_w_out[0])
    xt = _peer(x.reshape(T, D), norm_ffn_g[0], peer_w_q[0], peer_subkeys[0], peer_u[0], peer_v[0],
               norm_final_g, final_norm=False)
    q, k, v = _qkv(xt, norm_mix_g[1], od_w_qkv[0])
    aw = MOBA_HEADS * MOBA_HEAD_DIM
    o = _attention(q.reshape(B, S, aw), k.reshape(B, S, aw), v.reshape(B, S, aw), _bias_tiles(rel_bias))
    xt = _oproj(xt, o.reshape(T, aw), od_w_o[0])
    xt = _peer(xt, norm_ffn_g[1], peer_w_q[1], peer_subkeys[1], peer_u[1], peer_v[1],
               norm_final_g, final_norm=True)
    return xt.reshape(B, S, D)
```

```python
import functools
import math

import jax
import jax.numpy as jnp
from jax import lax
from jax.experimental import pallas as pl
from jax.experimental.pallas import tpu as pltpu

D_MODEL = 1024
A_CH = 512
CONV_W = 31
B_HEADS = 8
B_HEAD_DIM = 64
B_CH = B_HEADS * B_HEAD_DIM
SGU_CHUNK = 128
MIX_IN = 2 * A_CH + 2 * B_CH
MOBA_HEADS = 16
MOBA_HEAD_DIM = 64
MOBA_BLOCK = 256
MOBA_TOPK = 3
REL_BUCKETS = 32
REL_MAX_DIST = 2048
PEER_HEADS = 8
PEER_NKEYS = 128
PEER_N_EXPERTS = PEER_NKEYS * PEER_NKEYS
PEER_DKEY = 256
PEER_TOPK = 16
EPS = 1e-6
NEG = -1e30

LANES = 128
MXU_DTYPE = jnp.bfloat16
ROUTE_DTYPE = jnp.bfloat16
VMEM_LIMIT = 56 * 1024 * 1024

MIX_TS = 512
CONV_HALO = 32
PEER_TB = 512
PEER_EC = 512
MM_TM = 512
ATT_TQ = MOBA_BLOCK
BIAS_TILES = 8


def _rms(x, g):
    return x * lax.rsqrt(jnp.mean(x * x, axis=-1, keepdims=True) + EPS) * g


def _layer_norm(x, g, b):
    mu = jnp.mean(x, axis=-1, keepdims=True)
    var = jnp.mean(jnp.square(x - mu), axis=-1, keepdims=True)
    return (x - mu) * lax.rsqrt(var + EPS) * g + b


def _gelu(x):
    cdf = 0.5 * (1.0 + jnp.tanh(math.sqrt(2.0 / math.pi) * (x + 0.044715 * (x * x * x))))
    return x * cdf


def _mm(a, b):
    return jnp.dot(a.astype(MXU_DTYPE), b.astype(MXU_DTYPE), preferred_element_type=jnp.float32)


def _mm_nt(a, b):
    return lax.dot_general(a.astype(MXU_DTYPE), b.astype(MXU_DTYPE), (((1,), (1,)), ((), ())),
                           preferred_element_type=jnp.float32)


def _mixer_kernel(x_ref, g_ref, win_ref, cw_ref, cb_ref, clg_ref, clb_ref, slg_ref, slb_ref,
                  sw_ref, sb_ref, wout_ref, o_ref, abuf_ref, s_ref):
    ts = x_ref.shape[1]
    x = x_ref[0]
    h = _rms(x, g_ref[...])
    z = _mm(h, win_ref[...])
    a = z[:, :A_CH] * jax.nn.sigmoid(z[:, A_CH:2 * A_CH])

    @pl.when(pl.program_id(1) == 0)
    def _():
        abuf_ref[0:CONV_HALO, :] = jnp.zeros((CONV_HALO, A_CH), jnp.float32)

    @pl.when(pl.program_id(1) > 0)
    def _():
        abuf_ref[0:CONV_HALO, :] = abuf_ref[ts:ts + CONV_HALO, :]

    abuf_ref[CONV_HALO:CONV_HALO + ts, :] = a
    acc = jnp.broadcast_to(cb_ref[...], (ts, A_CH))
    for w in range(CONV_W):
        off = CONV_HALO - (CONV_W - 1) + w
        acc = acc + cw_ref[w:w + 1, :] * abuf_ref[off:off + ts, :]
    a_n = _layer_norm(acc, clg_ref[...], clb_ref[...])
    a_out = a_n * jax.nn.sigmoid(a_n)

    u = _gelu(z[:, 2 * A_CH:2 * A_CH + B_CH])
    v = _layer_norm(_gelu(z[:, 2 * A_CH + B_CH:]), slg_ref[...], slb_ref[...])
    row = lax.broadcasted_iota(jnp.int32, (SGU_CHUNK, SGU_CHUNK), 0)
    col = lax.broadcasted_iota(jnp.int32, (SGU_CHUNK, SGU_CHUNK), 1)
    low_head = lax.broadcasted_iota(jnp.int32, (SGU_CHUNK, LANES), 1) < B_HEAD_DIM
    wm = [jnp.where(col <= row, sw_ref[hh], 0.0).astype(MXU_DTYPE) for hh in range(B_HEADS)]
    for c in range(ts // SGU_CHUNK):
        for pr in range(B_HEADS // 2):
            vp = v[c * SGU_CHUNK:(c + 1) * SGU_CHUNK, pr * LANES:(pr + 1) * LANES].astype(MXU_DTYPE)
            s0 = jnp.dot(wm[2 * pr], vp, preferred_element_type=jnp.float32)
            s1 = jnp.dot(wm[2 * pr + 1], vp, preferred_element_type=jnp.float32)
            s_ref[c * SGU_CHUNK:(c + 1) * SGU_CHUNK, pr * LANES:(pr + 1) * LANES] = (
                jnp.where(low_head, s0, s1) + sb_ref[:, pr * LANES:(pr + 1) * LANES])
    bo = u * s_ref[...]
    out = _mm(a_out, wout_ref[0:A_CH, :]) + _mm(bo, wout_ref[A_CH:, :])
    o_ref[0] = x + out


def _mixer(x, g, w_in, conv_w, conv_b, cln_g, cln_b, sln_g, sln_b, sgu_w, sgu_b, w_out):
    B, S, D = x.shape
    ts = min(MIX_TS, S)
    assert S % ts == 0 and ts % SGU_CHUNK == 0
    row2 = lambda a: a.reshape(1, -1)
    sb_exp = jnp.repeat(sgu_b.T, B_HEAD_DIM, axis=1)
    const = lambda shape: pl.BlockSpec(shape, lambda b, s: (0,) * len(shape))
    return pl.pallas_call(
        _mixer_kernel,
        grid=(B, S // ts),
        in_specs=[
            pl.BlockSpec((1, ts, D), lambda b, s: (b, s, 0)),
            const((1, D)), const((D, MIX_IN)), const((CONV_W, A_CH)), const((1, A_CH)),
            const((1, A_CH)), const((1, A_CH)), const((1, B_CH)), const((1, B_CH)),
            const((B_HEADS, SGU_CHUNK, SGU_CHUNK)), const((SGU_CHUNK, B_CH)), const((A_CH + B_CH, D)),
        ],
        out_specs=pl.BlockSpec((1, ts, D), lambda b, s: (b, s, 0)),
        out_shape=jax.ShapeDtypeStruct((B, S, D), jnp.float32),
        scratch_shapes=[pltpu.VMEM((ts + CONV_HALO, A_CH), jnp.float32),
                        pltpu.VMEM((ts, B_CH), jnp.float32)],
        compiler_params=pltpu.CompilerParams(dimension_semantics=("arbitrary", "arbitrary"),
                                             vmem_limit_bytes=VMEM_LIMIT),
        name="mixer0",
    )(x, row2(g), w_in.astype(MXU_DTYPE), conv_w, row2(conv_b), row2(cln_g), row2(cln_b),
      row2(sln_g), row2(sln_b), sgu_w, sb_exp, w_out.astype(MXU_DTYPE))


def _candidate_pairs():
    return [(a, b) for a in range(PEER_TOPK) for b in range(PEER_TOPK) if (a + 1) * (b + 1) <= PEER_TOPK]


def _top16(s, iota_f, vals_ref, p, hh):
    work = s
    rank = jnp.full(s.shape, float(PEER_TOPK), jnp.float32)
    for r in range(PEER_TOPK):
        m = jnp.max(work, axis=0, keepdims=True)
        first = jnp.min(jnp.where(work == m, iota_f, float(PEER_NKEYS)), axis=0, keepdims=True)
        sel = iota_f == first
        rank = jnp.where(sel, float(r), rank)
        work = jnp.where(sel, -jnp.inf, work)
        vals_ref[p, r, pl.ds(hh, 1), :] = m
    return rank


def _gelu_sigmoid_form(x):
    k0 = -2.0 * math.sqrt(2.0 / math.pi) * math.log2(math.e)
    u = x * (k0 + (k0 * 0.044715) * (x * x))
    return x / (1.0 + jnp.exp2(u))


def _routed_chunk(a_ref, p_ref, chunk, rank2_ref, e2_ref, n_ref, e1_ref):
    ec, tb = a_ref.shape
    rows_per_tile = 16
    for g in range(ec // PEER_NKEYS):
        i = jnp.clip(chunk * (ec // PEER_NKEYS) + g, 0, PEER_NKEYS - 1)
        n_rows = [n_ref[hh, pl.ds(i, 1), :] for hh in range(PEER_HEADS)]
        e1_rows = [e1_ref[hh, pl.ds(i, 1), :] for hh in range(PEER_HEADS)]
        for lt in range(tb // LANES):
            cols = slice(lt * LANES, (lt + 1) * LANES)
            n_b = [jnp.broadcast_to(n_rows[hh][:, cols], (rows_per_tile, LANES)).astype(ROUTE_DTYPE)
                   for hh in range(PEER_HEADS)]
            e1_b = [jnp.broadcast_to(e1_rows[hh][:, cols], (rows_per_tile, LANES)).astype(ROUTE_DTYPE)
                    for hh in range(PEER_HEADS)]
            for sg in range(PEER_NKEYS // rows_per_tile):
                rows = slice(sg * rows_per_tile, (sg + 1) * rows_per_tile)
                wgt = None
                for hh in range(PEER_HEADS):
                    term = jnp.where(rank2_ref[hh, rows, cols] < n_b[hh], e2_ref[hh, rows, cols],
                                     jnp.zeros((), ROUTE_DTYPE)) * e1_b[hh]
                    wgt = term if wgt is None else wgt + term
                arows = slice(g * PEER_NKEYS + sg * rows_per_tile, g * PEER_NKEYS + (sg + 1) * rows_per_tile)
                act = _gelu_sigmoid_form(a_ref[arows, cols])
                p_ref[arows, cols] = (wgt * act.astype(ROUTE_DTYPE)).astype(p_ref.dtype)


def _peer_kernel(x_ref, g_ref, wq_ref, sk_ref, u_ref, vt_ref, gf_ref, o_ref,
                 ht_ref, yt_ref, q_ref, rank1_ref, rank2_ref, e1_ref, e2_ref, n_ref,
                 vals_ref, na_ref, a0_ref, a1_ref, p0_ref, p1_ref, *, final_norm):
    k = pl.program_id(1)
    tb = x_ref.shape[0]
    ec = a0_ref.shape[0]

    @pl.when(k == 0)
    def _route():
        h = _rms(x_ref[...], g_ref[...])
        ht = h.T.astype(MXU_DTYPE)
        ht_ref[...] = ht
        q_ref[...] = jnp.dot(wq_ref[...], ht, preferred_element_type=jnp.float32)
        yt_ref[...] = jnp.zeros_like(yt_ref)
        a1_ref[...] = jnp.zeros_like(a1_ref)
        p0_ref[...] = jnp.zeros_like(p0_ref)
        p1_ref[...] = jnp.zeros_like(p1_ref)
        iota_f = lax.broadcasted_iota(jnp.int32, (PEER_NKEYS, tb), 0).astype(jnp.float32)

        def head_scores(hh, carry):
            for p in range(2):
                idx = hh * 2 + p
                start = pl.multiple_of(idx * PEER_NKEYS, PEER_NKEYS)
                qhp = q_ref[pl.ds(start, PEER_NKEYS), :]
                s = jnp.dot(sk_ref[idx], qhp.astype(MXU_DTYPE), preferred_element_type=jnp.float32)
                rank = _top16(s, iota_f, vals_ref, p, hh)
                e = jnp.exp(s - jnp.max(s, axis=0, keepdims=True))
                if p == 0:
                    rank1_ref[hh] = rank
                    e1_ref[hh] = e
                else:
                    rank2_ref[hh] = rank.astype(ROUTE_DTYPE)
                    e2_ref[hh] = e.astype(ROUTE_DTYPE)
            return carry

        lax.fori_loop(0, PEER_HEADS, head_scores, 0)

        pairs = _candidate_pairs()
        v1 = [vals_ref[0, a] for a in range(PEER_TOPK)]
        v2 = [vals_ref[1, b] for b in range(PEER_TOPK)]
        cand = {ab: v1[ab[0]] + v2[ab[1]] for ab in pairs}
        pos = {ab: jnp.full((PEER_HEADS, tb), float((ab[0] + 1) * (ab[1] + 1) - 1), jnp.float32) for ab in pairs}
        for i, x_ab in enumerate(pairs):
            for y_ab in pairs[i + 1:]:
                (a, b), (a2, b2) = x_ab, y_ab
                if (a2 >= a and b2 >= b) or (a2 <= a and b2 <= b):
                    continue
                x_first = cand[x_ab] >= cand[y_ab]
                pos[y_ab] = pos[y_ab] + jnp.where(x_first, 1.0, 0.0)
                pos[x_ab] = pos[x_ab] + jnp.where(x_first, 0.0, 1.0)
        c00 = cand[(0, 0)]
        z = jnp.zeros((PEER_HEADS, tb), jnp.float32)
        n_a = [jnp.zeros((PEER_HEADS, tb), jnp.float32) for _ in range(PEER_TOPK)]
        for (a, b) in pairs:
            chosen = pos[(a, b)] < float(PEER_TOPK)
            n_a[a] = n_a[a] + jnp.where(chosen, 1.0, 0.0)
            z = z + jnp.where(chosen, jnp.exp(cand[(a, b)] - c00), 0.0)
        inv_z = 1.0 / z
        for a in range(PEER_TOPK):
            na_ref[a] = n_a[a]
        na_ref[PEER_TOPK] = inv_z

        def head_rows(hh, carry):
            rank1 = rank1_ref[hh]
            n_i = jnp.zeros((PEER_NKEYS, tb), jnp.float32)
            for a in range(PEER_TOPK):
                n_i = jnp.where(rank1 == float(a), na_ref[a, pl.ds(hh, 1), :], n_i)
            n_ref[hh] = n_i
            e1_ref[hh] = e1_ref[hh] * na_ref[PEER_TOPK, pl.ds(hh, 1), :]
            return carry

        lax.fori_loop(0, PEER_HEADS, head_rows, 0)

    routing = (rank2_ref, e2_ref, n_ref, e1_ref)
    ht = ht_ref[...]
    y_even = jnp.dot(vt_ref[:, 0:ec], p0_ref[...], preferred_element_type=jnp.float32)
    a0_ref[...] = jnp.dot(u_ref[0:ec, :], ht, preferred_element_type=jnp.float32)
    _routed_chunk(a1_ref, p1_ref, 2 * k - 1, *routing)
    y_odd = jnp.dot(vt_ref[:, ec:2 * ec], p1_ref[...], preferred_element_type=jnp.float32)
    a1_ref[...] = jnp.dot(u_ref[ec:2 * ec, :], ht, preferred_element_type=jnp.float32)
    _routed_chunk(a0_ref, p0_ref, 2 * k, *routing)
    yt_ref[...] += y_even + y_odd

    @pl.when(k == pl.num_programs(1) - 1)
    def _finish():
        out = x_ref[...] + yt_ref[...].T
        if final_norm:
            out = _rms(out, gf_ref[...])
        o_ref[...] = out


def _peer(x, g, w_q, subkeys, u_tab, v_tab, g_final, final_norm):
    T, D = x.shape
    tb = min(PEER_TB, T)
    ec = PEER_EC
    assert T % tb == 0 and PEER_N_EXPERTS % (2 * ec) == 0 and ec % PEER_NKEYS == 0
    n_pair = PEER_N_EXPERTS // (2 * ec)
    hq = PEER_HEADS * PEER_DKEY
    wq_t = w_q.T.astype(MXU_DTYPE)
    sk = subkeys.reshape(PEER_HEADS * 2, PEER_NKEYS, PEER_DKEY // 2).astype(MXU_DTYPE)
    u_b = u_tab.astype(MXU_DTYPE)
    vt_b = v_tab.T.astype(MXU_DTYPE)
    head_tile = (PEER_HEADS, PEER_NKEYS, tb)
    return pl.pallas_call(
        functools.partial(_peer_kernel, final_norm=final_norm),
        grid=(T // tb, n_pair + 1),
        in_specs=[
            pl.BlockSpec((tb, D), lambda t, k: (t, 0)),
            pl.BlockSpec((1, D), lambda t, k: (0, 0)),
            pl.BlockSpec((hq, D), lambda t, k: (0, 0)),
            pl.BlockSpec((PEER_HEADS * 2, PEER_NKEYS, PEER_DKEY // 2), lambda t, k: (0, 0, 0)),
            pl.BlockSpec((2 * ec, D), lambda t, k: (jnp.minimum(k, n_pair - 1), 0)),
            pl.BlockSpec((D, 2 * ec), lambda t, k: (0, jnp.maximum(k - 1, 0))),
            pl.BlockSpec((1, D), lambda t, k: (0, 0)),
        ],
        out_specs=pl.BlockSpec((tb, D), lambda t, k: (t, 0)),
        out_shape=jax.ShapeDtypeStruct((T, D), jnp.float32),
        scratch_shapes=[
            pltpu.VMEM((D, tb), MXU_DTYPE),
            pltpu.VMEM((D, tb), jnp.float32),
            pltpu.VMEM((hq, tb), jnp.float32),
            pltpu.VMEM(head_tile, jnp.float32),
            pltpu.VMEM(head_tile, ROUTE_DTYPE),
            pltpu.VMEM(head_tile, jnp.float32),
            pltpu.VMEM(head_tile, ROUTE_DTYPE),
            pltpu.VMEM(head_tile, jnp.float32),
            pltpu.VMEM((2, PEER_TOPK, PEER_HEADS, tb), jnp.float32),
            pltpu.VMEM((PEER_TOPK + 1, PEER_HEADS, tb), jnp.float32),
            pltpu.VMEM((ec, tb), jnp.float32),
            pltpu.VMEM((ec, tb), jnp.float32),
            pltpu.VMEM((ec, tb), MXU_DTYPE),
            pltpu.VMEM((ec, tb), MXU_DTYPE),
        ],
        compiler_params=pltpu.CompilerParams(dimension_semantics=("arbitrary", "arbitrary"),
                                             vmem_limit_bytes=VMEM_LIMIT),
        name="peer_final" if final_norm else "peer",
    )(x, g.reshape(1, D), wq_t, sk, u_b, vt_b, g_final.reshape(1, D))


def _qkv_kernel(x_ref, g_ref, w_ref, q_ref, k_ref, v_ref):
    h = _rms(x_ref[...], g_ref[...])
    qkv = _mm(h, w_ref[...])
    d = q_ref.shape[1]
    q_ref[...] = qkv[:, :d]
    k_ref[...] = qkv[:, d:2 * d]
    v_ref[...] = qkv[:, 2 * d:]


def _qkv(x, g, w_qkv):
    T, D = x.shape
    tm = min(MM_TM, T)
    aw = MOBA_HEADS * MOBA_HEAD_DIM
    out = jax.ShapeDtypeStruct((T, aw), jnp.float32)
    return pl.pallas_call(
        _qkv_kernel,
        grid=(T // tm,),
        in_specs=[pl.BlockSpec((tm, D), lambda i: (i, 0)),
                  pl.BlockSpec((1, D), lambda i: (0, 0)),
                  pl.BlockSpec((D, 3 * aw), lambda i: (0, 0))],
        out_specs=[pl.BlockSpec((tm, aw), lambda i: (i, 0))] * 3,
        out_shape=[out, out, out],
        compiler_params=pltpu.CompilerParams(dimension_semantics=("arbitrary",),
                                             vmem_limit_bytes=VMEM_LIMIT),
        name="qkv",
    )(x, g.reshape(1, D), w_qkv.astype(MXU_DTYPE))


def _oproj_kernel(x_ref, o_ref, w_ref, y_ref):
    y_ref[...] = x_ref[...] + _mm(o_ref[...], w_ref[...])


def _oproj(x, o, w_o):
    T, D = x.shape
    tm = min(MM_TM, T)
    aw = o.shape[1]
    return pl.pallas_call(
        _oproj_kernel,
        grid=(T // tm,),
        in_specs=[pl.BlockSpec((tm, D), lambda i: (i, 0)),
                  pl.BlockSpec((tm, aw), lambda i: (i, 0)),
                  pl.BlockSpec((aw, D), lambda i: (0, 0))],
        out_specs=pl.BlockSpec((tm, D), lambda i: (i, 0)),
        out_shape=jax.ShapeDtypeStruct((T, D), jnp.float32),
        compiler_params=pltpu.CompilerParams(dimension_semantics=("arbitrary",),
                                             vmem_limit_bytes=VMEM_LIMIT),
        name="oproj",
    )(x, o, w_o.astype(MXU_DTYPE))


def _t5_bucket(dist):
    n = jnp.maximum(dist, 0)
    max_exact = REL_BUCKETS // 2
    nf = jnp.maximum(n, 1).astype(jnp.float32)
    log_b = max_exact + (jnp.log(nf / max_exact) / math.log(REL_MAX_DIST / max_exact)
                         * (REL_BUCKETS - max_exact)).astype(jnp.int32)
    return jnp.where(n < max_exact, n, jnp.minimum(log_b, REL_BUCKETS - 1))


def _bias_kernel(rb_ref, o_ref):
    h = pl.program_id(0)
    dblk = pl.program_id(1)
    key = lax.broadcasted_iota(jnp.int32, (MOBA_BLOCK, MOBA_BLOCK), 0)
    qry = lax.broadcasted_iota(jnp.int32, (MOBA_BLOCK, MOBA_BLOCK), 1)
    bucket = _t5_bucket(dblk * MOBA_BLOCK + qry - key)
    val = jnp.zeros((MOBA_BLOCK, MOBA_BLOCK), jnp.float32)
    for b in range(REL_BUCKETS):
        val = jnp.where(bucket == b, rb_ref[b, h], val)
    o_ref[0, 0] = val


def _bias_tiles(rel_bias):
    last_tile_min_dist = (BIAS_TILES - 1) * MOBA_BLOCK - (MOBA_BLOCK - 1)
    max_exact = REL_BUCKETS // 2
    first_last_bucket = max_exact * (REL_MAX_DIST / max_exact) ** ((REL_BUCKETS - 1 - max_exact) / (REL_BUCKETS - max_exact))
    assert last_tile_min_dist > first_last_bucket + 1
    return pl.pallas_call(
        _bias_kernel,
        grid=(MOBA_HEADS, BIAS_TILES),
        in_specs=[pl.BlockSpec(memory_space=pltpu.SMEM)],
        out_specs=pl.BlockSpec((1, 1, MOBA_BLOCK, MOBA_BLOCK), lambda h, d: (h, d, 0, 0)),
        out_shape=jax.ShapeDtypeStruct((MOBA_HEADS, BIAS_TILES, MOBA_BLOCK, MOBA_BLOCK), jnp.float32),
        compiler_params=pltpu.CompilerParams(dimension_semantics=("arbitrary", "arbitrary")),
        name="t5_bias_tiles",
    )(rel_bias)


def _attn_kernel(q_ref, k_ref, v_ref, bias_ref, o_ref,
                 kb_ref, vt_ref, kmean_ref, sel_ref, m_ref, l_ref, acc_ref):
    qi = pl.program_id(2)
    S = k_ref.shape[1]
    n_blk = S // MOBA_BLOCK
    tq = q_ref.shape[1]

    @pl.when(qi == 0)
    def _prep():
        for n in range(n_blk):
            kblk = k_ref[0, n * MOBA_BLOCK:(n + 1) * MOBA_BLOCK, :]
            kb_ref[n * MOBA_BLOCK:(n + 1) * MOBA_BLOCK, :] = kblk.astype(MXU_DTYPE)
            kmean_ref[n:n + 1, :] = jnp.mean(kblk, axis=0, keepdims=True)
            vt_ref[:, n * MOBA_BLOCK:(n + 1) * MOBA_BLOCK] = (
                v_ref[0, n * MOBA_BLOCK:(n + 1) * MOBA_BLOCK, :].T.astype(MXU_DTYPE))

    own = qi
    q = q_ref[0]
    low_head = lax.broadcasted_iota(jnp.int32, (tq, LANES), 1) < MOBA_HEAD_DIM
    q_heads = [jnp.where(low_head, q, 0.0), jnp.where(low_head, 0.0, q)]
    blk = lax.broadcasted_iota(jnp.int32, (n_blk, tq), 0)
    past = blk < own
    key_i = lax.broadcasted_iota(jnp.int32, (MOBA_BLOCK, tq), 0)
    qry_i = lax.broadcasted_iota(jnp.int32, (MOBA_BLOCK, tq), 1)
    qs = []
    k_own = kb_ref[pl.ds(pl.multiple_of(own * MOBA_BLOCK, MOBA_BLOCK), MOBA_BLOCK), :]
    vt_own = vt_ref[:, pl.ds(pl.multiple_of(own * MOBA_BLOCK, MOBA_BLOCK), MOBA_BLOCK)]
    for e in range(2):
        gate = jnp.where(past, _mm_nt(kmean_ref[...], q_heads[e]), -jnp.inf)
        beaten = jnp.zeros((n_blk, tq), jnp.float32)
        for m in range(n_blk):
            gm = gate[m:m + 1, :]
            beats = (gm > gate) | ((gm == gate) & (blk > m))
            beaten = beaten + jnp.where(beats, 1.0, 0.0)
        sel_ref[e] = jnp.where(past & (beaten < float(MOBA_TOPK)), 1.0, 0.0)

        qe = (q_heads[e] * (MOBA_HEAD_DIM ** -0.5)).astype(MXU_DTYPE)
        qs.append(qe)
        st = _mm_nt(k_own, qe) + bias_ref[e, 0]
        st = jnp.where(key_i <= qry_i, st, NEG)
        m0 = jnp.max(st, axis=0, keepdims=True)
        p = jnp.exp(st - m0)
        m_ref[e] = m0
        l_ref[e] = jnp.sum(p, axis=0, keepdims=True)
        acc_ref[e] = jnp.dot(vt_own, p.astype(MXU_DTYPE), preferred_element_type=jnp.float32)

    def past_block(n, carry):
        start = pl.multiple_of(n * MOBA_BLOCK, MOBA_BLOCK)
        k_n = kb_ref[pl.ds(start, MOBA_BLOCK), :]
        vt_n = vt_ref[:, pl.ds(start, MOBA_BLOCK)]
        tile = jnp.minimum(own - n, BIAS_TILES - 1)
        for e in range(2):
            st = _mm_nt(k_n, qs[e]) + bias_ref[e, tile]
            st = jnp.where(sel_ref[e, pl.ds(n, 1), :] > 0.0, st, NEG)
            m_old = m_ref[e]
            m_new = jnp.maximum(m_old, jnp.max(st, axis=0, keepdims=True))
            alpha = jnp.exp(m_old - m_new)
            p = jnp.exp(st - m_new)
            m_ref[e] = m_new
            l_ref[e] = alpha * l_ref[e] + jnp.sum(p, axis=0, keepdims=True)
            acc_ref[e] = alpha * acc_ref[e] + jnp.dot(vt_n, p.astype(MXU_DTYPE),
                                                      preferred_element_type=jnp.float32)
        return carry

    lax.fori_loop(0, own, past_block, 0)

    o0 = acc_ref[0] / l_ref[0]
    o1 = acc_ref[1] / l_ref[1]
    ot = jnp.concatenate([o0[:MOBA_HEAD_DIM], o1[MOBA_HEAD_DIM:]], axis=0)
    o_ref[0] = ot.T


def _attention(q, k, v, bias_tiles):
    B, S, aw = q.shape
    tq = ATT_TQ
    assert S % MOBA_BLOCK == 0 and tq == MOBA_BLOCK
    n_blk = S // MOBA_BLOCK
    n_pair = aw // LANES
    return pl.pallas_call(
        _attn_kernel,
        grid=(n_pair, B, S // tq),
        in_specs=[
            pl.BlockSpec((1, tq, LANES), lambda hp, b, qi: (b, qi, hp)),
            pl.BlockSpec((1, S, LANES), lambda hp, b, qi: (b, 0, hp)),
            pl.BlockSpec((1, S, LANES), lambda hp, b, qi: (b, 0, hp)),
            pl.BlockSpec((2, BIAS_TILES, MOBA_BLOCK, MOBA_BLOCK), lambda hp, b, qi: (hp, 0, 0, 0)),
        ],
        out_specs=pl.BlockSpec((1, tq, LANES), lambda hp, b, qi: (b, qi, hp)),
        out_shape=jax.ShapeDtypeStruct((B, S, aw), jnp.float32),
        scratch_shapes=[
            pltpu.VMEM((S, LANES), MXU_DTYPE),
            pltpu.VMEM((LANES, S), MXU_DTYPE),
            pltpu.VMEM((n_blk, LANES), jnp.float32),
            pltpu.VMEM((2, n_blk, tq), jnp.float32),
            pltpu.VMEM((2, 1, tq), jnp.float32),
            pltpu.VMEM((2, 1, tq), jnp.float32),
            pltpu.VMEM((2, LANES, tq), jnp.float32),
        ],
        compiler_params=pltpu.CompilerParams(dimension_semantics=("arbitrary", "arbitrary", "arbitrary"),
                                             vmem_limit_bytes=VMEM_LIMIT),
        name="moba_attention",
    )(q, k, v, bias_tiles)


def kernel(x, ev_w_in, ev_conv_w, ev_conv_b, ev_conv_ln_g, ev_conv_ln_b, ev_sgu_ln_g, ev_sgu_ln_b, ev_sgu_w, ev_sgu_b, ev_w_out, od_w_qkv, od_w_o, rel_bias, peer_w_q, peer_subkeys, peer_u, peer_v, norm_mix_g, norm_ffn_g, norm_final_g):
    B, S, D = x.shape
    T = B * S
    x = _mixer(x, norm_mix_g[0], ev_w_in[0], ev_conv_w[0], ev_conv_b[0], ev_conv_ln_g[0], ev_conv_ln_b[0],
               ev_sgu_ln_g[0], ev_sgu_ln_b[0], ev_sgu_w[0], ev_sgu_b[0], ev_w_out[0])
    xt = _peer(x.reshape(T, D), norm_ffn_g[0], peer_w_q[0], peer_subkeys[0], peer_u[0], peer_v[0],
               norm_final_g, final_norm=False)
    q, k, v = _qkv(xt, norm_mix_g[1], od_w_qkv[0])
    aw = MOBA_HEADS * MOBA_HEAD_DIM
    o = _attention(q.reshape(B, S, aw), k.reshape(B, S, aw), v.reshape(B, S, aw), _bias_tiles(rel_bias))
    xt = _oproj(xt, o.reshape(T, aw), od_w_o[0])
    xt = _peer(xt, norm_ffn_g[1], peer_w_q[1], peer_subkeys[1], peer_u[1], peer_v[1],
               norm_final_g, final_norm=True)
    return xt.reshape(B, S, D)
```

```python
import functools
import math

import jax
import jax.numpy as jnp
from jax import lax
from jax.experimental import pallas as pl
from jax.experimental.pallas import tpu as pltpu

D_MODEL = 1024
A_CH = 512
CONV_W = 31
B_HEADS = 8
B_HEAD_DIM = 64
B_CH = B_HEADS * B_HEAD_DIM
SGU_CHUNK = 128
MIX_IN = 2 * A_CH + 2 * B_CH
MOBA_HEADS = 16
MOBA_HEAD_DIM = 64
MOBA_BLOCK = 256
MOBA_TOPK = 3
REL_BUCKETS = 32
REL_MAX_DIST = 2048
PEER_HEADS = 8
PEER_NKEYS = 128
PEER_N_EXPERTS = PEER_NKEYS * PEER_NKEYS
PEER_DKEY = 256
PEER_TOPK = 16
EPS = 1e-6
NEG = -1e30
LOG2E = math.log2(math.e)

LANES = 128
MXU_DTYPE = jnp.bfloat16
ROUTE_DTYPE = jnp.bfloat16
VMEM_LIMIT = 56 * 1024 * 1024

MIX_TS = 512
CONV_HALO = 32
PEER_TB = 512
PEER_EC = 512
MM_TM = 512
ATT_TQ = MOBA_BLOCK
BIAS_TILES = 8


def _rms(x, g):
    return x * lax.rsqrt(jnp.mean(x * x, axis=-1, keepdims=True) + EPS) * g


def _layer_norm(x, g, b):
    mu = jnp.mean(x, axis=-1, keepdims=True)
    var = jnp.mean(jnp.square(x - mu), axis=-1, keepdims=True)
    return (x - mu) * lax.rsqrt(var + EPS) * g + b


def _gelu(x):
    cdf = 0.5 * (1.0 + jnp.tanh(math.sqrt(2.0 / math.pi) * (x + 0.044715 * (x * x * x))))
    return x * cdf


def _mm(a, b):
    return jnp.dot(a.astype(MXU_DTYPE), b.astype(MXU_DTYPE), preferred_element_type=jnp.float32)


def _mm_nt(a, b):
    return lax.dot_general(a.astype(MXU_DTYPE), b.astype(MXU_DTYPE), (((1,), (1,)), ((), ())),
                           preferred_element_type=jnp.float32)


def _mixer_kernel(x_ref, g_ref, win_ref, cw_ref, cb_ref, clg_ref, clb_ref, slg_ref, slb_ref,
                  sw_ref, sb_ref, wout_ref, o_ref, abuf_ref, s_ref):
    ts = x_ref.shape[1]
    x = x_ref[0]
    h = _rms(x, g_ref[...])
    z = _mm(h, win_ref[...])
    a = z[:, :A_CH] * jax.nn.sigmoid(z[:, A_CH:2 * A_CH])

    @pl.when(pl.program_id(1) == 0)
    def _():
        abuf_ref[0:CONV_HALO, :] = jnp.zeros((CONV_HALO, A_CH), jnp.float32)

    @pl.when(pl.program_id(1) > 0)
    def _():
        abuf_ref[0:CONV_HALO, :] = abuf_ref[ts:ts + CONV_HALO, :]

    abuf_ref[CONV_HALO:CONV_HALO + ts, :] = a
    acc = jnp.broadcast_to(cb_ref[...], (ts, A_CH))
    for w in range(CONV_W):
        off = CONV_HALO - (CONV_W - 1) + w
        acc = acc + cw_ref[w:w + 1, :] * abuf_ref[off:off + ts, :]
    a_n = _layer_norm(acc, clg_ref[...], clb_ref[...])
    a_out = a_n * jax.nn.sigmoid(a_n)

    u = _gelu(z[:, 2 * A_CH:2 * A_CH + B_CH])
    v = _layer_norm(_gelu(z[:, 2 * A_CH + B_CH:]), slg_ref[...], slb_ref[...])
    row = lax.broadcasted_iota(jnp.int32, (SGU_CHUNK, SGU_CHUNK), 0)
    col = lax.broadcasted_iota(jnp.int32, (SGU_CHUNK, SGU_CHUNK), 1)
    low_head = lax.broadcasted_iota(jnp.int32, (SGU_CHUNK, LANES), 1) < B_HEAD_DIM
    wm = [jnp.where(col <= row, sw_ref[hh], 0.0).astype(MXU_DTYPE) for hh in range(B_HEADS)]
    for c in range(ts // SGU_CHUNK):
        for pr in range(B_HEADS // 2):
            vp = v[c * SGU_CHUNK:(c + 1) * SGU_CHUNK, pr * LANES:(pr + 1) * LANES].astype(MXU_DTYPE)
            s0 = jnp.dot(wm[2 * pr], vp, preferred_element_type=jnp.float32)
            s1 = jnp.dot(wm[2 * pr + 1], vp, preferred_element_type=jnp.float32)
            s_ref[c * SGU_CHUNK:(c + 1) * SGU_CHUNK, pr * LANES:(pr + 1) * LANES] = (
                jnp.where(low_head, s0, s1) + sb_ref[:, pr * LANES:(pr + 1) * LANES])
    bo = u * s_ref[...]
    out = _mm(a_out, wout_ref[0:A_CH, :]) + _mm(bo, wout_ref[A_CH:, :])
    o_ref[0] = x + out


def _mixer(x, g, w_in, conv_w, conv_b, cln_g, cln_b, sln_g, sln_b, sgu_w, sgu_b, w_out):
    B, S, D = x.shape
    ts = min(MIX_TS, S)
    assert S % ts == 0 and ts % SGU_CHUNK == 0
    row2 = lambda a: a.reshape(1, -1)
    sb_exp = jnp.repeat(sgu_b.T, B_HEAD_DIM, axis=1)
    const = lambda shape: pl.BlockSpec(shape, lambda b, s: (0,) * len(shape))
    return pl.pallas_call(
        _mixer_kernel,
        grid=(B, S // ts),
        in_specs=[
            pl.BlockSpec((1, ts, D), lambda b, s: (b, s, 0)),
            const((1, D)), const((D, MIX_IN)), const((CONV_W, A_CH)), const((1, A_CH)),
            const((1, A_CH)), const((1, A_CH)), const((1, B_CH)), const((1, B_CH)),
            const((B_HEADS, SGU_CHUNK, SGU_CHUNK)), const((SGU_CHUNK, B_CH)), const((A_CH + B_CH, D)),
        ],
        out_specs=pl.BlockSpec((1, ts, D), lambda b, s: (b, s, 0)),
        out_shape=jax.ShapeDtypeStruct((B, S, D), jnp.float32),
        scratch_shapes=[pltpu.VMEM((ts + CONV_HALO, A_CH), jnp.float32),
                        pltpu.VMEM((ts, B_CH), jnp.float32)],
        compiler_params=pltpu.CompilerParams(dimension_semantics=("arbitrary", "arbitrary"),
                                             vmem_limit_bytes=VMEM_LIMIT),
        name="mixer0",
    )(x, row2(g), w_in.astype(MXU_DTYPE), conv_w, row2(conv_b), row2(cln_g), row2(cln_b),
      row2(sln_g), row2(sln_b), sgu_w, sb_exp, w_out.astype(MXU_DTYPE))


def _candidate_pairs():
    return [(a, b) for a in range(PEER_TOPK) for b in range(PEER_TOPK) if (a + 1) * (b + 1) <= PEER_TOPK]


def _top16(s, iota_f, vals_ref, p, hh):
    work = s
    rank = jnp.full(s.shape, float(PEER_TOPK), jnp.float32)
    for r in range(PEER_TOPK):
        m = jnp.max(work, axis=0, keepdims=True)
        first = jnp.min(jnp.where(work == m, iota_f, float(PEER_NKEYS)), axis=0, keepdims=True)
        sel = iota_f == first
        rank = jnp.where(sel, float(r), rank)
        work = jnp.where(sel, -jnp.inf, work)
        vals_ref[p, r, pl.ds(hh, 1), :] = m
    return rank


def _gelu_sigmoid_form(x):
    k0 = -2.0 * math.sqrt(2.0 / math.pi) * math.log2(math.e)
    u = x * (k0 + (k0 * 0.044715) * (x * x))
    return x / (1.0 + jnp.exp2(u))


def _routed_chunk(a_ref, p_ref, chunk, rank2_ref, e2_ref, n_ref, e1_ref):
    ec, tb = a_ref.shape
    rows_per_tile = 16
    for g in range(ec // PEER_NKEYS):
        i = jnp.clip(chunk * (ec // PEER_NKEYS) + g, 0, PEER_NKEYS - 1)
        n_rows = [n_ref[hh, pl.ds(i, 1), :] for hh in range(PEER_HEADS)]
        e1_rows = [e1_ref[hh, pl.ds(i, 1), :] for hh in range(PEER_HEADS)]
        for lt in range(tb // LANES):
            cols = slice(lt * LANES, (lt + 1) * LANES)
            n_b = [jnp.broadcast_to(n_rows[hh][:, cols], (rows_per_tile, LANES)).astype(ROUTE_DTYPE)
                   for hh in range(PEER_HEADS)]
            e1_b = [jnp.broadcast_to(e1_rows[hh][:, cols], (rows_per_tile, LANES)).astype(ROUTE_DTYPE)
                    for hh in range(PEER_HEADS)]
            for sg in range(PEER_NKEYS // rows_per_tile):
                rows = slice(sg * rows_per_tile, (sg + 1) * rows_per_tile)
                wgt = None
                for hh in range(PEER_HEADS):
                    term = jnp.where(rank2_ref[hh, rows, cols] < n_b[hh], e2_ref[hh, rows, cols],
                                     jnp.zeros((), ROUTE_DTYPE)) * e1_b[hh]
                    wgt = term if wgt is None else wgt + term
                arows = slice(g * PEER_NKEYS + sg * rows_per_tile, g * PEER_NKEYS + (sg + 1) * rows_per_tile)
                act = _gelu_sigmoid_form(a_ref[arows, cols])
                p_ref[arows, cols] = (wgt * act.astype(ROUTE_DTYPE)).astype(p_ref.dtype)


def _peer_kernel(x_ref, g_ref, wq_ref, sk_ref, u_ref, vt_ref, gf_ref, o_ref,
                 ht_ref, yt_ref, q_ref, rank1_ref, rank2_ref, e1_ref, e2_ref, n_ref,
                 vals_ref, na_ref, a0_ref, a1_ref, p0_ref, p1_ref, *, final_norm):
    k = pl.program_id(1)
    tb = x_ref.shape[0]
    ec = a0_ref.shape[0]

    @pl.when(k == 0)
    def _route():
        h = _rms(x_ref[...], g_ref[...])
        ht = h.T.astype(MXU_DTYPE)
        ht_ref[...] = ht
        q_ref[...] = jnp.dot(wq_ref[...], ht, preferred_element_type=jnp.float32)
        yt_ref[...] = jnp.zeros_like(yt_ref)
        a1_ref[...] = jnp.zeros_like(a1_ref)
        p0_ref[...] = jnp.zeros_like(p0_ref)
        p1_ref[...] = jnp.zeros_like(p1_ref)
        iota_f = lax.broadcasted_iota(jnp.int32, (PEER_NKEYS, tb), 0).astype(jnp.float32)

        def head_scores(hh, carry):
            for p in range(2):
                idx = hh * 2 + p
                start = pl.multiple_of(idx * PEER_NKEYS, PEER_NKEYS)
                qhp = q_ref[pl.ds(start, PEER_NKEYS), :]
                s = jnp.dot(sk_ref[idx], qhp.astype(MXU_DTYPE), preferred_element_type=jnp.float32)
                rank = _top16(s, iota_f, vals_ref, p, hh)
                e = jnp.exp(s - jnp.max(s, axis=0, keepdims=True))
                if p == 0:
                    rank1_ref[hh] = rank
                    e1_ref[hh] = e
                else:
                    rank2_ref[hh] = rank.astype(ROUTE_DTYPE)
                    e2_ref[hh] = e.astype(ROUTE_DTYPE)
            return carry

        lax.fori_loop(0, PEER_HEADS, head_scores, 0)

        pairs = _candidate_pairs()
        v1 = [vals_ref[0, a] for a in range(PEER_TOPK)]
        v2 = [vals_ref[1, b] for b in range(PEER_TOPK)]
        cand = {ab: v1[ab[0]] + v2[ab[1]] for ab in pairs}
        pos = {ab: jnp.full((PEER_HEADS, tb), float((ab[0] + 1) * (ab[1] + 1) - 1), jnp.float32) for ab in pairs}
        for i, x_ab in enumerate(pairs):
            for y_ab in pairs[i + 1:]:
                (a, b), (a2, b2) = x_ab, y_ab
                if (a2 >= a and b2 >= b) or (a2 <= a and b2 <= b):
                    continue
                x_first = cand[x_ab] >= cand[y_ab]
                pos[y_ab] = pos[y_ab] + jnp.where(x_first, 1.0, 0.0)
                pos[x_ab] = pos[x_ab] + jnp.where(x_first, 0.0, 1.0)
        c00 = cand[(0, 0)]
        z = jnp.zeros((PEER_HEADS, tb), jnp.float32)
        n_a = [jnp.zeros((PEER_HEADS, tb), jnp.float32) for _ in range(PEER_TOPK)]
        for (a, b) in pairs:
            chosen = pos[(a, b)] < float(PEER_TOPK)
            n_a[a] = n_a[a] + jnp.where(chosen, 1.0, 0.0)
            z = z + jnp.where(chosen, jnp.exp(cand[(a, b)] - c00), 0.0)
        inv_z = 1.0 / z
        for a in range(PEER_TOPK):
            na_ref[a] = n_a[a]
        na_ref[PEER_TOPK] = inv_z

        def head_rows(hh, carry):
            rank1 = rank1_ref[hh]
            n_i = jnp.zeros((PEER_NKEYS, tb), jnp.float32)
            for a in range(PEER_TOPK):
                n_i = jnp.where(rank1 == float(a), na_ref[a, pl.ds(hh, 1), :], n_i)
            n_ref[hh] = n_i
            e1_ref[hh] = e1_ref[hh] * na_ref[PEER_TOPK, pl.ds(hh, 1), :]
            return carry

        lax.fori_loop(0, PEER_HEADS, head_rows, 0)

    routing = (rank2_ref, e2_ref, n_ref, e1_ref)
    ht = ht_ref[...]
    y_even = jnp.dot(vt_ref[:, 0:ec], p0_ref[...], preferred_element_type=jnp.float32)
    a0_ref[...] = jnp.dot(u_ref[0:ec, :], ht, preferred_element_type=jnp.float32)
    _routed_chunk(a1_ref, p1_ref, 2 * k - 1, *routing)
    y_odd = jnp.dot(vt_ref[:, ec:2 * ec], p1_ref[...], preferred_element_type=jnp.float32)
    a1_ref[...] = jnp.dot(u_ref[ec:2 * ec, :], ht, preferred_element_type=jnp.float32)
    _routed_chunk(a0_ref, p0_ref, 2 * k, *routing)
    yt_ref[...] += y_even + y_odd

    @pl.when(k == pl.num_programs(1) - 1)
    def _finish():
        out = x_ref[...] + yt_ref[...].T
        if final_norm:
            out = _rms(out, gf_ref[...])
        o_ref[...] = out


def _peer(x, g, w_q, subkeys, u_tab, v_tab, g_final, final_norm):
    T, D = x.shape
    tb = min(PEER_TB, T)
    ec = PEER_EC
    assert T % tb == 0 and PEER_N_EXPERTS % (2 * ec) == 0 and ec % PEER_NKEYS == 0
    n_pair = PEER_N_EXPERTS // (2 * ec)
    hq = PEER_HEADS * PEER_DKEY
    wq_t = w_q.T.astype(MXU_DTYPE)
    sk = subkeys.reshape(PEER_HEADS * 2, PEER_NKEYS, PEER_DKEY // 2).astype(MXU_DTYPE)
    u_b = u_tab.astype(MXU_DTYPE)
    vt_b = v_tab.T.astype(MXU_DTYPE)
    head_tile = (PEER_HEADS, PEER_NKEYS, tb)
    return pl.pallas_call(
        functools.partial(_peer_kernel, final_norm=final_norm),
        grid=(T // tb, n_pair + 1),
        in_specs=[
            pl.BlockSpec((tb, D), lambda t, k: (t, 0)),
            pl.BlockSpec((1, D), lambda t, k: (0, 0)),
            pl.BlockSpec((hq, D), lambda t, k: (0, 0)),
            pl.BlockSpec((PEER_HEADS * 2, PEER_NKEYS, PEER_DKEY // 2), lambda t, k: (0, 0, 0)),
            pl.BlockSpec((2 * ec, D), lambda t, k: (jnp.minimum(k, n_pair - 1), 0)),
            pl.BlockSpec((D, 2 * ec), lambda t, k: (0, jnp.maximum(k - 1, 0))),
            pl.BlockSpec((1, D), lambda t, k: (0, 0)),
        ],
        out_specs=pl.BlockSpec((tb, D), lambda t, k: (t, 0)),
        out_shape=jax.ShapeDtypeStruct((T, D), jnp.float32),
        scratch_shapes=[
            pltpu.VMEM((D, tb), MXU_DTYPE),
            pltpu.VMEM((D, tb), jnp.float32),
            pltpu.VMEM((hq, tb), jnp.float32),
            pltpu.VMEM(head_tile, jnp.float32),
            pltpu.VMEM(head_tile, ROUTE_DTYPE),
            pltpu.VMEM(head_tile, jnp.float32),
            pltpu.VMEM(head_tile, ROUTE_DTYPE),
            pltpu.VMEM(head_tile, jnp.float32),
            pltpu.VMEM((2, PEER_TOPK, PEER_HEADS, tb), jnp.float32),
            pltpu.VMEM((PEER_TOPK + 1, PEER_HEADS, tb), jnp.float32),
            pltpu.VMEM((ec, tb), jnp.float32),
            pltpu.VMEM((ec, tb), jnp.float32),
            pltpu.VMEM((ec, tb), MXU_DTYPE),
            pltpu.VMEM((ec, tb), MXU_DTYPE),
        ],
        compiler_params=pltpu.CompilerParams(dimension_semantics=("arbitrary", "arbitrary"),
                                             vmem_limit_bytes=VMEM_LIMIT),
        name="peer_final" if final_norm else "peer",
    )(x, g.reshape(1, D), wq_t, sk, u_b, vt_b, g_final.reshape(1, D))


def _qkv_kernel(x_ref, g_ref, w_ref, q_ref, k_ref, v_ref):
    h = _rms(x_ref[...], g_ref[...])
    qkv = _mm(h, w_ref[...])
    d = q_ref.shape[1]
    q_ref[...] = qkv[:, :d]
    k_ref[...] = qkv[:, d:2 * d]
    v_ref[...] = qkv[:, 2 * d:]


def _qkv(x, g, w_qkv):
    T, D = x.shape
    tm = min(MM_TM, T)
    aw = MOBA_HEADS * MOBA_HEAD_DIM
    out = jax.ShapeDtypeStruct((T, aw), jnp.float32)
    return pl.pallas_call(
        _qkv_kernel,
        grid=(T // tm,),
        in_specs=[pl.BlockSpec((tm, D), lambda i: (i, 0)),
                  pl.BlockSpec((1, D), lambda i: (0, 0)),
                  pl.BlockSpec((D, 3 * aw), lambda i: (0, 0))],
        out_specs=[pl.BlockSpec((tm, aw), lambda i: (i, 0))] * 3,
        out_shape=[out, out, out],
        compiler_params=pltpu.CompilerParams(dimension_semantics=("arbitrary",),
                                             vmem_limit_bytes=VMEM_LIMIT),
        name="qkv",
    )(x, g.reshape(1, D), w_qkv.astype(MXU_DTYPE))


def _oproj_kernel(x_ref, o_ref, w_ref, y_ref):
    y_ref[...] = x_ref[...] + _mm(o_ref[...], w_ref[...])


def _oproj(x, o, w_o):
    T, D = x.shape
    tm = min(MM_TM, T)
    aw = o.shape[1]
    return pl.pallas_call(
        _oproj_kernel,
        grid=(T // tm,),
        in_specs=[pl.BlockSpec((tm, D), lambda i: (i, 0)),
                  pl.BlockSpec((tm, aw), lambda i: (i, 0)),
                  pl.BlockSpec((aw, D), lambda i: (0, 0))],
        out_specs=pl.BlockSpec((tm, D), lambda i: (i, 0)),
        out_shape=jax.ShapeDtypeStruct((T, D), jnp.float32),
        compiler_params=pltpu.CompilerParams(dimension_semantics=("arbitrary",),
                                             vmem_limit_bytes=VMEM_LIMIT),
        name="oproj",
    )(x, o, w_o.astype(MXU_DTYPE))


def _t5_bucket(dist):
    n = jnp.maximum(dist, 0)
    max_exact = REL_BUCKETS // 2
    nf = jnp.maximum(n, 1).astype(jnp.float32)
    log_b = max_exact + (jnp.log(nf / max_exact) / math.log(REL_MAX_DIST / max_exact)
                         * (REL_BUCKETS - max_exact)).astype(jnp.int32)
    return jnp.where(n < max_exact, n, jnp.minimum(log_b, REL_BUCKETS - 1))


def _bias_kernel(rb_ref, o_ref):
    h = pl.program_id(0)
    dblk = pl.program_id(1)
    key = lax.broadcasted_iota(jnp.int32, (MOBA_BLOCK, MOBA_BLOCK), 0)
    qry = lax.broadcasted_iota(jnp.int32, (MOBA_BLOCK, MOBA_BLOCK), 1)
    dist = dblk * MOBA_BLOCK + qry - key
    bucket = _t5_bucket(dist)
    val = jnp.zeros((MOBA_BLOCK, MOBA_BLOCK), jnp.float32)
    for b in range(REL_BUCKETS):
        val = jnp.where(bucket == b, rb_ref[b, h], val)
    o_ref[0, 0] = jnp.where(dist >= 0, val * LOG2E, NEG)


def _bias_tiles(rel_bias):
    last_tile_min_dist = (BIAS_TILES - 1) * MOBA_BLOCK - (MOBA_BLOCK - 1)
    max_exact = REL_BUCKETS // 2
    first_last_bucket = max_exact * (REL_MAX_DIST / max_exact) ** ((REL_BUCKETS - 1 - max_exact) / (REL_BUCKETS - max_exact))
    assert last_tile_min_dist > first_last_bucket + 1
    return pl.pallas_call(
        _bias_kernel,
        grid=(MOBA_HEADS, BIAS_TILES),
        in_specs=[pl.BlockSpec(memory_space=pltpu.SMEM)],
        out_specs=pl.BlockSpec((1, 1, MOBA_BLOCK, MOBA_BLOCK), lambda h, d: (h, d, 0, 0)),
        out_shape=jax.ShapeDtypeStruct((MOBA_HEADS, BIAS_TILES, MOBA_BLOCK, MOBA_BLOCK), jnp.float32),
        compiler_params=pltpu.CompilerParams(dimension_semantics=("arbitrary", "arbitrary")),
        name="t5_bias_tiles",
    )(rel_bias)


def _attn_kernel(qa_ref, qb_ref, k_ref, v_ref, bias_ref, o_ref,
                 kb_ref, vt_ref, kmean_ref, qs_ref, sel_ref, m_ref, l_ref, acc_ref):
    j = pl.program_id(2)
    S = k_ref.shape[1]
    n_blk = S // MOBA_BLOCK
    tq = qa_ref.shape[1]

    @pl.when(j == 0)
    def _prep():
        for n in range(n_blk):
            kblk = k_ref[0, n * MOBA_BLOCK:(n + 1) * MOBA_BLOCK, :]
            kb_ref[n * MOBA_BLOCK:(n + 1) * MOBA_BLOCK, :] = kblk.astype(MXU_DTYPE)
            kmean_ref[n:n + 1, :] = jnp.mean(kblk, axis=0, keepdims=True)
            vt_ref[:, n * MOBA_BLOCK:(n + 1) * MOBA_BLOCK] = (
                v_ref[0, n * MOBA_BLOCK:(n + 1) * MOBA_BLOCK, :].T.astype(MXU_DTYPE))

    owns = (j, n_blk - 1 - j)
    low_head = lax.broadcasted_iota(jnp.int32, (tq, LANES), 1) < MOBA_HEAD_DIM
    blk = lax.broadcasted_iota(jnp.int32, (n_blk, tq), 0)
    for t, q_ref in enumerate((qa_ref, qb_ref)):
        q = q_ref[0]
        past = blk < owns[t]
        for e in range(2):
            q_e = jnp.where(low_head, q, 0.0) if e == 0 else jnp.where(low_head, 0.0, q)
            gate = jnp.where(past, _mm_nt(kmean_ref[...], q_e), -jnp.inf)
            beaten = jnp.zeros((n_blk, tq), jnp.float32)
            for m in range(n_blk):
                gm = gate[m:m + 1, :]
                beats = (gm > gate) | ((gm == gate) & (blk > m))
                beaten = beaten + jnp.where(beats, 1.0, 0.0)
            chosen = (past & (beaten < float(MOBA_TOPK))) | (blk == owns[t])
            sel_ref[t, e] = jnp.where(chosen, 1.0, 0.0)
            qs_ref[t, e] = (q_e * (MOBA_HEAD_DIM ** -0.5 * LOG2E)).astype(MXU_DTYPE)
            m_ref[t, e] = jnp.full((1, tq), NEG, jnp.float32)
            l_ref[t, e] = jnp.zeros((1, tq), jnp.float32)
            acc_ref[t, e] = jnp.zeros((LANES, tq), jnp.float32)

    for u in range(n_blk + 1):
        is_a = u <= j
        t = jnp.where(is_a, 0, 1)
        n = jnp.where(is_a, j - u, n_blk - u)
        own = jnp.where(is_a, j, n_blk - 1 - j)
        start = pl.multiple_of(n * MOBA_BLOCK, MOBA_BLOCK)
        k_n = kb_ref[pl.ds(start, MOBA_BLOCK), :]
        vt_n = vt_ref[:, pl.ds(start, MOBA_BLOCK)]
        tile = jnp.minimum(own - n, BIAS_TILES - 1)
        for e in range(2):
            st = _mm_nt(k_n, qs_ref[t, e]) + bias_ref[e, tile]
            st = jnp.where(sel_ref[t, e, pl.ds(n, 1), :] > 0.0, st, NEG)
            m_old = m_ref[t, e]
            m_new = jnp.maximum(m_old, jnp.max(st, axis=0, keepdims=True))
            alpha = jnp.exp2(m_old - m_new)
            p = jnp.exp2(st - m_new)
            m_ref[t, e] = m_new
            l_ref[t, e] = alpha * l_ref[t, e] + jnp.sum(p, axis=0, keepdims=True)
            acc_ref[t, e] = alpha * acc_ref[t, e] + jnp.dot(vt_n, p.astype(MXU_DTYPE),
                                                            preferred_element_type=jnp.float32)

    for t in range(2):
        o0 = acc_ref[t, 0] / l_ref[t, 0]
        o1 = acc_ref[t, 1] / l_ref[t, 1]
        ot = jnp.concatenate([o0[:MOBA_HEAD_DIM], o1[MOBA_HEAD_DIM:]], axis=0)
        row0 = pl.multiple_of(owns[t] * tq, tq)
        o_ref[0, pl.ds(row0, tq), :] = ot.T


def _attention(q, k, v, bias_tiles):
    B, S, aw = q.shape
    tq = ATT_TQ
    assert S % (2 * MOBA_BLOCK) == 0 and tq == MOBA_BLOCK
    n_blk = S // MOBA_BLOCK
    n_pair = aw // LANES
    return pl.pallas_call(
        _attn_kernel,
        grid=(n_pair, B, n_blk // 2),
        in_specs=[
            pl.BlockSpec((1, tq, LANES), lambda hp, b, j: (b, j, hp)),
            pl.BlockSpec((1, tq, LANES), lambda hp, b, j: (b, n_blk - 1 - j, hp)),
            pl.BlockSpec((1, S, LANES), lambda hp, b, j: (b, 0, hp)),
            pl.BlockSpec((1, S, LANES), lambda hp, b, j: (b, 0, hp)),
            pl.BlockSpec((2, BIAS_TILES, MOBA_BLOCK, MOBA_BLOCK), lambda hp, b, j: (hp, 0, 0, 0)),
        ],
        out_specs=pl.BlockSpec((1, S, LANES), lambda hp, b, j: (b, 0, hp)),
        out_shape=jax.ShapeDtypeStruct((B, S, aw), jnp.float32),
        scratch_shapes=[
            pltpu.VMEM((S, LANES), MXU_DTYPE),
            pltpu.VMEM((LANES, S), MXU_DTYPE),
            pltpu.VMEM((n_blk, LANES), jnp.float32),
            pltpu.VMEM((2, 2, tq, LANES), MXU_DTYPE),
            pltpu.VMEM((2, 2, n_blk, tq), jnp.float32),
            pltpu.VMEM((2, 2, 1, tq), jnp.float32),
            pltpu.VMEM((2, 2, 1, tq), jnp.float32),
            pltpu.VMEM((2, 2, LANES, tq), jnp.float32),
        ],
        compiler_params=pltpu.CompilerParams(dimension_semantics=("arbitrary", "arbitrary", "arbitrary"),
                                             vmem_limit_bytes=VMEM_LIMIT),
        name="moba_attention",
    )(q, q, k, v, bias_tiles)


def kernel(x, ev_w_in, ev_conv_w, ev_conv_b, ev_conv_ln_g, ev_conv_ln_b, ev_sgu_ln_g, ev_sgu_ln_b, ev_sgu_w, ev_sgu_b, ev_w_out, od_w_qkv, od_w_o, rel_bias, peer_w_q, peer_subkeys, peer_u, peer_v, norm_mix_g, norm_ffn_g, norm_final_g):
    B, S, D = x.shape
    T = B * S
    x = _mixer(x, norm_mix_g[0], ev_w_in[0], ev_conv_w[0], ev_conv_b[0], ev_conv_ln_g[0], ev_conv_ln_b[0],
               ev_sgu_ln_g[0], ev_sgu_ln_b[0], ev_sgu_w[0], ev_sgu_b[0], ev_w_out[0])
    xt = _peer(x.reshape(T, D), norm_ffn_g[0], peer_w_q[0], peer_subkeys[0], peer_u[0], peer_v[0],
               norm_final_g, final_norm=False)
    q, k, v = _qkv(xt, norm_mix_g[1], od_w_qkv[0])
    aw = MOBA_HEADS * MOBA_HEAD_DIM
    o = _attention(q.reshape(B, S, aw), k.reshape(B, S, aw), v.reshape(B, S, aw), _bias_tiles(rel_bias))
    xt = _oproj(xt, o.reshape(T, aw), od_w_o[0])
    xt = _peer(xt, norm_ffn_g[1], peer_w_q[1], peer_subkeys[1], peer_u[1], peer_v[1],
               norm_final_g, final_norm=True)
    return xt.reshape(B, S, D)
```

```python
import functools
import math

import jax
import jax.numpy as jnp
from jax import lax
from jax.experimental import pallas as pl
from jax.experimental.pallas import tpu as pltpu

D_MODEL = 1024
A_CH = 512
CONV_W = 31
B_HEADS = 8
B_HEAD_DIM = 64
B_CH = B_HEADS * B_HEAD_DIM
SGU_CHUNK = 128
MIX_IN = 2 * A_CH + 2 * B_CH
MOBA_HEADS = 16
MOBA_HEAD_DIM = 64
MOBA_BLOCK = 256
MOBA_TOPK = 3
REL_BUCKETS = 32
REL_MAX_DIST = 2048
PEER_HEADS = 8
PEER_NKEYS = 128
PEER_N_EXPERTS = PEER_NKEYS * PEER_NKEYS
PEER_DKEY = 256
PEER_TOPK = 16
EPS = 1e-6
NEG = -1e30
LOG2E = math.log2(math.e)

LANES = 128
MXU_DTYPE = jnp.bfloat16
ROUTE_DTYPE = jnp.bfloat16
VMEM_LIMIT = 56 * 1024 * 1024

MIX_TS = 512
CONV_HALO = 32
PEER_TB = 512
PEER_EC = 512
MM_TM = 512
ATT_TQ = MOBA_BLOCK
BIAS_TILES = 8


def _rms(x, g):
    return x * lax.rsqrt(jnp.mean(x * x, axis=-1, keepdims=True) + EPS) * g


def _layer_norm(x, g, b):
    mu = jnp.mean(x, axis=-1, keepdims=True)
    var = jnp.mean(jnp.square(x - mu), axis=-1, keepdims=True)
    return (x - mu) * lax.rsqrt(var + EPS) * g + b


def _gelu(x):
    cdf = 0.5 * (1.0 + jnp.tanh(math.sqrt(2.0 / math.pi) * (x + 0.044715 * (x * x * x))))
    return x * cdf


def _mm(a, b):
    return jnp.dot(a.astype(MXU_DTYPE), b.astype(MXU_DTYPE), preferred_element_type=jnp.float32)


def _mm_nt(a, b):
    return lax.dot_general(a.astype(MXU_DTYPE), b.astype(MXU_DTYPE), (((1,), (1,)), ((), ())),
                           preferred_element_type=jnp.float32)


def _mixer_kernel(x_ref, g_ref, win_ref, cw_ref, cb_ref, clg_ref, clb_ref, slg_ref, slb_ref,
                  sw_ref, sb_ref, wout_ref, o_ref, abuf_ref, s_ref):
    ts = x_ref.shape[1]
    x = x_ref[0]
    h = _rms(x, g_ref[...])
    z = _mm(h, win_ref[...])
    a = z[:, :A_CH] * jax.nn.sigmoid(z[:, A_CH:2 * A_CH])

    @pl.when(pl.program_id(1) == 0)
    def _():
        abuf_ref[0:CONV_HALO, :] = jnp.zeros((CONV_HALO, A_CH), jnp.float32)

    @pl.when(pl.program_id(1) > 0)
    def _():
        abuf_ref[0:CONV_HALO, :] = abuf_ref[ts:ts + CONV_HALO, :]

    abuf_ref[CONV_HALO:CONV_HALO + ts, :] = a
    acc = jnp.broadcast_to(cb_ref[...], (ts, A_CH))
    for w in range(CONV_W):
        off = CONV_HALO - (CONV_W - 1) + w
        acc = acc + cw_ref[w:w + 1, :] * abuf_ref[off:off + ts, :]
    a_n = _layer_norm(acc, clg_ref[...], clb_ref[...])
    a_out = a_n * jax.nn.sigmoid(a_n)

    u = _gelu(z[:, 2 * A_CH:2 * A_CH + B_CH])
    v = _layer_norm(_gelu(z[:, 2 * A_CH + B_CH:]), slg_ref[...], slb_ref[...])
    row = lax.broadcasted_iota(jnp.int32, (SGU_CHUNK, SGU_CHUNK), 0)
    col = lax.broadcasted_iota(jnp.int32, (SGU_CHUNK, SGU_CHUNK), 1)
    low_head = lax.broadcasted_iota(jnp.int32, (SGU_CHUNK, LANES), 1) < B_HEAD_DIM
    wm = [jnp.where(col <= row, sw_ref[hh], 0.0).astype(MXU_DTYPE) for hh in range(B_HEADS)]
    for c in range(ts // SGU_CHUNK):
        for pr in range(B_HEADS // 2):
            vp = v[c * SGU_CHUNK:(c + 1) * SGU_CHUNK, pr * LANES:(pr + 1) * LANES].astype(MXU_DTYPE)
            s0 = jnp.dot(wm[2 * pr], vp, preferred_element_type=jnp.float32)
            s1 = jnp.dot(wm[2 * pr + 1], vp, preferred_element_type=jnp.float32)
            s_ref[c * SGU_CHUNK:(c + 1) * SGU_CHUNK, pr * LANES:(pr + 1) * LANES] = (
                jnp.where(low_head, s0, s1) + sb_ref[:, pr * LANES:(pr + 1) * LANES])
    bo = u * s_ref[...]
    out = _mm(a_out, wout_ref[0:A_CH, :]) + _mm(bo, wout_ref[A_CH:, :])
    o_ref[0] = x + out


def _mixer(x, g, w_in, conv_w, conv_b, cln_g, cln_b, sln_g, sln_b, sgu_w, sgu_b, w_out):
    B, S, D = x.shape
    ts = min(MIX_TS, S)
    assert S % ts == 0 and ts % SGU_CHUNK == 0
    row2 = lambda a: a.reshape(1, -1)
    sb_exp = jnp.repeat(sgu_b.T, B_HEAD_DIM, axis=1)
    const = lambda shape: pl.BlockSpec(shape, lambda b, s: (0,) * len(shape))
    return pl.pallas_call(
        _mixer_kernel,
        grid=(B, S // ts),
        in_specs=[
            pl.BlockSpec((1, ts, D), lambda b, s: (b, s, 0)),
            const((1, D)), const((D, MIX_IN)), const((CONV_W, A_CH)), const((1, A_CH)),
            const((1, A_CH)), const((1, A_CH)), const((1, B_CH)), const((1, B_CH)),
            const((B_HEADS, SGU_CHUNK, SGU_CHUNK)), const((SGU_CHUNK, B_CH)), const((A_CH + B_CH, D)),
        ],
        out_specs=pl.BlockSpec((1, ts, D), lambda b, s: (b, s, 0)),
        out_shape=jax.ShapeDtypeStruct((B, S, D), jnp.float32),
        scratch_shapes=[pltpu.VMEM((ts + CONV_HALO, A_CH), jnp.float32),
                        pltpu.VMEM((ts, B_CH), jnp.float32)],
        compiler_params=pltpu.CompilerParams(dimension_semantics=("arbitrary", "arbitrary"),
                                             vmem_limit_bytes=VMEM_LIMIT),
        name="mixer0",
    )(x, row2(g), w_in.astype(MXU_DTYPE), conv_w, row2(conv_b), row2(cln_g), row2(cln_b),
      row2(sln_g), row2(sln_b), sgu_w, sb_exp, w_out.astype(MXU_DTYPE))


def _candidate_pairs():
    return [(a, b) for a in range(PEER_TOPK) for b in range(PEER_TOPK) if (a + 1) * (b + 1) <= PEER_TOPK]


def _top16(s, iota_f, vals_ref, p, hh):
    work = s
    rank = jnp.full(s.shape, float(PEER_TOPK), jnp.float32)
    for r in range(PEER_TOPK):
        m = jnp.max(work, axis=0, keepdims=True)
        first = jnp.min(jnp.where(work == m, iota_f, float(PEER_NKEYS)), axis=0, keepdims=True)
        sel = iota_f == first
        rank = jnp.where(sel, float(r), rank)
        work = jnp.where(sel, -jnp.inf, work)
        vals_ref[p, r, pl.ds(hh, 1), :] = m
    return rank


def _gelu_sigmoid_form(x):
    k0 = -2.0 * math.sqrt(2.0 / math.pi) * math.log2(math.e)
    u = x * (k0 + (k0 * 0.044715) * (x * x))
    return x / (1.0 + jnp.exp2(u))


TOP16_MARK = 2.0 ** 127


def _top16_no_ties(s, vals_ref, p, hh):
    work = s
    for r in range(PEER_TOPK):
        m = jnp.max(work, axis=0, keepdims=True)
        work = jnp.where(work == m, -(1.0 + (PEER_TOPK - 1 - r) / PEER_TOPK) * TOP16_MARK, work)
        vals_ref[p, r, pl.ds(hh, 1), :] = m
    marked = work <= -TOP16_MARK
    rank = jnp.where(marked, (2.0 * PEER_TOPK - 1.0) + work * (PEER_TOPK / TOP16_MARK), float(PEER_TOPK))
    n_marked = jnp.sum(jnp.where(marked, 1.0, 0.0), axis=0, keepdims=True)
    ok = jnp.max(jnp.abs(n_marked - float(PEER_TOPK))) == 0.0
    return rank, ok


def _bf16_pair_words(x):
    hi = pltpu.bitcast(x.astype(jnp.bfloat16).astype(jnp.float32), jnp.int32)
    return hi | lax.shift_right_logical(hi, jnp.full(hi.shape, 16, jnp.int32))


def _routed_chunk(a_ref, p_ref, chunk, rank2_ref, e2_ref, n_ref, e1_ref):
    ec, tb = a_ref.shape
    rows_per_tile = 16
    for g in range(ec // PEER_NKEYS):
        i = jnp.clip(chunk * (ec // PEER_NKEYS) + g, 0, PEER_NKEYS - 1)
        n_rows = [n_ref[hh, pl.ds(i, 1), :] for hh in range(PEER_HEADS)]
        e1_rows = [e1_ref[hh, pl.ds(i, 1), :] for hh in range(PEER_HEADS)]
        for lt in range(tb // LANES):
            cols = slice(lt * LANES, (lt + 1) * LANES)
            n_b = [pltpu.bitcast(jnp.broadcast_to(n_rows[hh][:, cols], (8, LANES)), ROUTE_DTYPE)
                   for hh in range(PEER_HEADS)]
            e1_b = [pltpu.bitcast(jnp.broadcast_to(e1_rows[hh][:, cols], (8, LANES)), ROUTE_DTYPE)
                    for hh in range(PEER_HEADS)]
            for sg in range(PEER_NKEYS // rows_per_tile):
                rows = slice(sg * rows_per_tile, (sg + 1) * rows_per_tile)
                wgt = None
                for hh in range(PEER_HEADS):
                    term = jnp.where(rank2_ref[hh, rows, cols] < n_b[hh], e2_ref[hh, rows, cols],
                                     jnp.zeros((), ROUTE_DTYPE)) * e1_b[hh]
                    wgt = term if wgt is None else wgt + term
                arows = slice(g * PEER_NKEYS + sg * rows_per_tile, g * PEER_NKEYS + (sg + 1) * rows_per_tile)
                act = _gelu_sigmoid_form(a_ref[arows, cols])
                p_ref[arows, cols] = (wgt * act.astype(ROUTE_DTYPE)).astype(p_ref.dtype)


def _peer_kernel(x_ref, g_ref, wq_ref, sk_ref, u_ref, vt_ref, gf_ref, o_ref,
                 ht_ref, yt_ref, q_ref, rank1_ref, rank2_ref, e1_ref, e2_ref, n_ref, e1w_ref,
                 vals_ref, na_ref, a0_ref, a1_ref, p0_ref, p1_ref, *, final_norm):
    k = pl.program_id(1)
    tb = x_ref.shape[0]
    ec = a0_ref.shape[0]

    @pl.when(k == 0)
    def _route():
        h = _rms(x_ref[...], g_ref[...])
        ht = h.T.astype(MXU_DTYPE)
        ht_ref[...] = ht
        q_ref[...] = jnp.dot(wq_ref[...], ht, preferred_element_type=jnp.float32)
        yt_ref[...] = jnp.zeros_like(yt_ref)
        a1_ref[...] = jnp.zeros_like(a1_ref)
        p0_ref[...] = jnp.zeros_like(p0_ref)
        p1_ref[...] = jnp.zeros_like(p1_ref)
        iota_f = lax.broadcasted_iota(jnp.int32, (PEER_NKEYS, tb), 0).astype(jnp.float32)

        def head_scores(hh, carry):
            scores = []
            all_ok = None
            for p in range(2):
                idx = hh * 2 + p
                start = pl.multiple_of(idx * PEER_NKEYS, PEER_NKEYS)
                qhp = q_ref[pl.ds(start, PEER_NKEYS), :]
                s = jnp.dot(sk_ref[idx], qhp.astype(MXU_DTYPE), preferred_element_type=jnp.float32)
                s = s + 0.0
                scores.append(s)
                rank, ok = _top16_no_ties(s, vals_ref, p, hh)
                all_ok = ok if all_ok is None else jnp.logical_and(all_ok, ok)
                e = jnp.exp(s - vals_ref[p, 0, pl.ds(hh, 1), :])
                if p == 0:
                    rank1_ref[hh] = rank
                    e1_ref[hh] = e
                else:
                    rank2_ref[hh] = rank.astype(ROUTE_DTYPE)
                    e2_ref[hh] = e.astype(ROUTE_DTYPE)

            @pl.when(jnp.logical_not(all_ok))
            def _exact():
                rank1_ref[hh] = _top16(scores[0], iota_f, vals_ref, 0, hh)
                rank2_ref[hh] = _top16(scores[1], iota_f, vals_ref, 1, hh).astype(ROUTE_DTYPE)
            return carry

        lax.fori_loop(0, PEER_HEADS, head_scores, 0)

        pairs = _candidate_pairs()
        v1 = [vals_ref[0, a] for a in range(PEER_TOPK)]
        v2 = [vals_ref[1, b] for b in range(PEER_TOPK)]
        cand = {ab: v1[ab[0]] + v2[ab[1]] for ab in pairs}
        pos = {ab: jnp.full((PEER_HEADS, tb), float((ab[0] + 1) * (ab[1] + 1) - 1), jnp.float32) for ab in pairs}
        for i, x_ab in enumerate(pairs):
            for y_ab in pairs[i + 1:]:
                (a, b), (a2, b2) = x_ab, y_ab
                if (a2 >= a and b2 >= b) or (a2 <= a and b2 <= b):
                    continue
                x_first = cand[x_ab] >= cand[y_ab]
                pos[y_ab] = pos[y_ab] + jnp.where(x_first, 1.0, 0.0)
                pos[x_ab] = pos[x_ab] + jnp.where(x_first, 0.0, 1.0)
        c00 = cand[(0, 0)]
        z = jnp.zeros((PEER_HEADS, tb), jnp.float32)
        n_a = [jnp.zeros((PEER_HEADS, tb), jnp.float32) for _ in range(PEER_TOPK)]
        for (a, b) in pairs:
            chosen = pos[(a, b)] < float(PEER_TOPK)
            n_a[a] = n_a[a] + jnp.where(chosen, 1.0, 0.0)
            z = z + jnp.where(chosen, jnp.exp(cand[(a, b)] - c00), 0.0)
        inv_z = 1.0 / z
        for a in range(PEER_TOPK):
            na_ref[a] = n_a[a]
        na_ref[PEER_TOPK] = inv_z

        def head_rows(hh, carry):
            rank1 = rank1_ref[hh]
            n_i = jnp.zeros((PEER_NKEYS, tb), jnp.float32)
            for a in range(PEER_TOPK):
                n_i = jnp.where(rank1 == float(a), na_ref[a, pl.ds(hh, 1), :], n_i)
            n_ref[hh] = _bf16_pair_words(n_i)
            e1w_ref[hh] = _bf16_pair_words(e1_ref[hh] * na_ref[PEER_TOPK, pl.ds(hh, 1), :])
            return carry

        lax.fori_loop(0, PEER_HEADS, head_rows, 0)

    routing = (rank2_ref, e2_ref, n_ref, e1w_ref)
    ht = ht_ref[...]
    y_even = jnp.dot(vt_ref[:, 0:ec], p0_ref[...], preferred_element_type=jnp.float32)
    a0_ref[...] = jnp.dot(u_ref[0:ec, :], ht, preferred_element_type=jnp.float32)
    _routed_chunk(a1_ref, p1_ref, 2 * k - 1, *routing)
    y_odd = jnp.dot(vt_ref[:, ec:2 * ec], p1_ref[...], preferred_element_type=jnp.float32)
    a1_ref[...] = jnp.dot(u_ref[ec:2 * ec, :], ht, preferred_element_type=jnp.float32)
    _routed_chunk(a0_ref, p0_ref, 2 * k, *routing)
    yt_ref[...] += y_even + y_odd

    @pl.when(k == pl.num_programs(1) - 1)
    def _finish():
        out = x_ref[...] + yt_ref[...].T
        if final_norm:
            out = _rms(out, gf_ref[...])
        o_ref[...] = out


def _peer(x, g, w_q, subkeys, u_tab, v_tab, g_final, final_norm):
    T, D = x.shape
    tb = min(PEER_TB, T)
    ec = PEER_EC
    assert T % tb == 0 and PEER_N_EXPERTS % (2 * ec) == 0 and ec % PEER_NKEYS == 0
    n_pair = PEER_N_EXPERTS // (2 * ec)
    hq = PEER_HEADS * PEER_DKEY
    wq_t = w_q.T.astype(MXU_DTYPE)
    sk = subkeys.reshape(PEER_HEADS * 2, PEER_NKEYS, PEER_DKEY // 2).astype(MXU_DTYPE)
    u_b = u_tab.astype(MXU_DTYPE)
    vt_b = v_tab.T.astype(MXU_DTYPE)
    head_tile = (PEER_HEADS, PEER_NKEYS, tb)
    return pl.pallas_call(
        functools.partial(_peer_kernel, final_norm=final_norm),
        grid=(T // tb, n_pair + 1),
        in_specs=[
            pl.BlockSpec((tb, D), lambda t, k: (t, 0)),
            pl.BlockSpec((1, D), lambda t, k: (0, 0)),
            pl.BlockSpec((hq, D), lambda t, k: (0, 0)),
            pl.BlockSpec((PEER_HEADS * 2, PEER_NKEYS, PEER_DKEY // 2), lambda t, k: (0, 0, 0)),
            pl.BlockSpec((2 * ec, D), lambda t, k: (jnp.minimum(k, n_pair - 1), 0)),
            pl.BlockSpec((D, 2 * ec), lambda t, k: (0, jnp.maximum(k - 1, 0))),
            pl.BlockSpec((1, D), lambda t, k: (0, 0)),
        ],
        out_specs=pl.BlockSpec((tb, D), lambda t, k: (t, 0)),
        out_shape=jax.ShapeDtypeStruct((T, D), jnp.float32),
        scratch_shapes=[
            pltpu.VMEM((D, tb), MXU_DTYPE),
            pltpu.VMEM((D, tb), jnp.float32),
            pltpu.VMEM((hq, tb), jnp.float32),
            pltpu.VMEM(head_tile, jnp.float32),
            pltpu.VMEM(head_tile, ROUTE_DTYPE),
            pltpu.VMEM(head_tile, jnp.float32),
            pltpu.VMEM(head_tile, ROUTE_DTYPE),
            pltpu.VMEM(head_tile, jnp.int32),
            pltpu.VMEM(head_tile, jnp.int32),
            pltpu.VMEM((2, PEER_TOPK, PEER_HEADS, tb), jnp.float32),
            pltpu.VMEM((PEER_TOPK + 1, PEER_HEADS, tb), jnp.float32),
            pltpu.VMEM((ec, tb), jnp.float32),
            pltpu.VMEM((ec, tb), jnp.float32),
            pltpu.VMEM((ec, tb), MXU_DTYPE),
            pltpu.VMEM((ec, tb), MXU_DTYPE),
        ],
        compiler_params=pltpu.CompilerParams(dimension_semantics=("arbitrary", "arbitrary"),
                                             vmem_limit_bytes=VMEM_LIMIT),
        name="peer_final" if final_norm else "peer",
    )(x, g.reshape(1, D), wq_t, sk, u_b, vt_b, g_final.reshape(1, D))


def _qkv_kernel(x_ref, g_ref, w_ref, q_ref, k_ref, v_ref):
    h = _rms(x_ref[...], g_ref[...])
    qkv = _mm(h, w_ref[...])
    d = q_ref.shape[1]
    q_ref[...] = qkv[:, :d]
    k_ref[...] = qkv[:, d:2 * d]
    v_ref[...] = qkv[:, 2 * d:]


def _qkv(x, g, w_qkv):
    T, D = x.shape
    tm = min(MM_TM, T)
    aw = MOBA_HEADS * MOBA_HEAD_DIM
    out = jax.ShapeDtypeStruct((T, aw), jnp.float32)
    return pl.pallas_call(
        _qkv_kernel,
        grid=(T // tm,),
        in_specs=[pl.BlockSpec((tm, D), lambda i: (i, 0)),
                  pl.BlockSpec((1, D), lambda i: (0, 0)),
                  pl.BlockSpec((D, 3 * aw), lambda i: (0, 0))],
        out_specs=[pl.BlockSpec((tm, aw), lambda i: (i, 0))] * 3,
        out_shape=[out, out, out],
        compiler_params=pltpu.CompilerParams(dimension_semantics=("arbitrary",),
                                             vmem_limit_bytes=VMEM_LIMIT),
        name="qkv",
    )(x, g.reshape(1, D), w_qkv.astype(MXU_DTYPE))


def _oproj_kernel(x_ref, o_ref, w_ref, y_ref):
    y_ref[...] = x_ref[...] + _mm(o_ref[...], w_ref[...])


def _oproj(x, o, w_o):
    T, D = x.shape
    tm = min(MM_TM, T)
    aw = o.shape[1]
    return pl.pallas_call(
        _oproj_kernel,
        grid=(T // tm,),
        in_specs=[pl.BlockSpec((tm, D), lambda i: (i, 0)),
                  pl.BlockSpec((tm, aw), lambda i: (i, 0)),
                  pl.BlockSpec((aw, D), lambda i: (0, 0))],
        out_specs=pl.BlockSpec((tm, D), lambda i: (i, 0)),
        out_shape=jax.ShapeDtypeStruct((T, D), jnp.float32),
        compiler_params=pltpu.CompilerParams(dimension_semantics=("arbitrary",),
                                             vmem_limit_bytes=VMEM_LIMIT),
        name="oproj",
    )(x, o, w_o.astype(MXU_DTYPE))


def _t5_bucket(dist):
    n = jnp.maximum(dist, 0)
    max_exact = REL_BUCKETS // 2
    nf = jnp.maximum(n, 1).astype(jnp.float32)
    log_b = max_exact + (jnp.log(nf / max_exact) / math.log(REL_MAX_DIST / max_exact)
                         * (REL_BUCKETS - max_exact)).astype(jnp.int32)
    return jnp.where(n < max_exact, n, jnp.minimum(log_b, REL_BUCKETS - 1))


def _bias_kernel(rb_ref, o_ref):
    h = pl.program_id(0)
    dblk = pl.program_id(1)
    key = lax.broadcasted_iota(jnp.int32, (MOBA_BLOCK, MOBA_BLOCK), 0)
    qry = lax.broadcasted_iota(jnp.int32, (MOBA_BLOCK, MOBA_BLOCK), 1)
    dist = dblk * MOBA_BLOCK + qry - key
    bucket = _t5_bucket(dist)
    val = jnp.zeros((MOBA_BLOCK, MOBA_BLOCK), jnp.float32)
    for b in range(REL_BUCKETS):
        val = jnp.where(bucket == b, rb_ref[b, h], val)
    o_ref[0, 0] = jnp.where(dist >= 0, val * LOG2E, NEG)


def _bias_tiles(rel_bias):
    last_tile_min_dist = (BIAS_TILES - 1) * MOBA_BLOCK - (MOBA_BLOCK - 1)
    max_exact = REL_BUCKETS // 2
    first_last_bucket = max_exact * (REL_MAX_DIST / max_exact) ** ((REL_BUCKETS - 1 - max_exact) / (REL_BUCKETS - max_exact))
    assert last_tile_min_dist > first_last_bucket + 1
    return pl.pallas_call(
        _bias_kernel,
        grid=(MOBA_HEADS, BIAS_TILES),
        in_specs=[pl.BlockSpec(memory_space=pltpu.SMEM)],
        out_specs=pl.BlockSpec((1, 1, MOBA_BLOCK, MOBA_BLOCK), lambda h, d: (h, d, 0, 0)),
        out_shape=jax.ShapeDtypeStruct((MOBA_HEADS, BIAS_TILES, MOBA_BLOCK, MOBA_BLOCK), jnp.float32),
        compiler_params=pltpu.CompilerParams(dimension_semantics=("arbitrary", "arbitrary")),
        name="t5_bias_tiles",
    )(rel_bias)


def _attn_kernel(qa_ref, qb_ref, k_ref, v_ref, bias_ref, o_ref,
                 kb_ref, vt_ref, kmean_ref, qs_ref, sel_ref, m_ref, l_ref, acc_ref):
    j = pl.program_id(2)
    S = k_ref.shape[1]
    n_blk = S // MOBA_BLOCK
    tq = qa_ref.shape[1]

    @pl.when(j == 0)
    def _prep():
        for n in range(n_blk):
            kblk = k_ref[0, n * MOBA_BLOCK:(n + 1) * MOBA_BLOCK, :]
            kb_ref[n * MOBA_BLOCK:(n + 1) * MOBA_BLOCK, :] = kblk.astype(MXU_DTYPE)
            kmean_ref[n:n + 1, :] = jnp.mean(kblk, axis=0, keepdims=True)
            vt_ref[:, n * MOBA_BLOCK:(n + 1) * MOBA_BLOCK] = (
                v_ref[0, n * MOBA_BLOCK:(n + 1) * MOBA_BLOCK, :].T.astype(MXU_DTYPE))

    owns = (j, n_blk - 1 - j)
    low_head = lax.broadcasted_iota(jnp.int32, (tq, LANES), 1) < MOBA_HEAD_DIM
    blk = lax.broadcasted_iota(jnp.int32, (n_blk, tq), 0)
    for t, q_ref in enumerate((qa_ref, qb_ref)):
        q = q_ref[0]
        past = blk < owns[t]
        for e in range(2):
            q_e = jnp.where(low_head, q, 0.0) if e == 0 else jnp.where(low_head, 0.0, q)
            gate = jnp.where(past, _mm_nt(kmean_ref[...], q_e), -jnp.inf)
            beaten = jnp.zeros((n_blk, tq), jnp.float32)
            for m in range(n_blk):
                gm = gate[m:m + 1, :]
                beats = (gm > gate) | ((gm == gate) & (blk > m))
                beaten = beaten + jnp.where(beats, 1.0, 0.0)
            chosen = (past & (beaten < float(MOBA_TOPK))) | (blk == owns[t])
            sel_ref[t, e] = jnp.where(chosen, 1.0, 0.0)
            qs_ref[t, e] = (q_e * (MOBA_HEAD_DIM ** -0.5 * LOG2E)).astype(MXU_DTYPE)
            m_ref[t, e] = jnp.full((1, tq), NEG, jnp.float32)
            l_ref[t, e] = jnp.zeros((1, tq), jnp.float32)
            acc_ref[t, e] = jnp.zeros((LANES, tq), jnp.float32)

    for u in range(n_blk + 1):
        is_a = u <= j
        t = jnp.where(is_a, 0, 1)
        n = jnp.where(is_a, j - u, n_blk - u)
        own = jnp.where(is_a, j, n_blk - 1 - j)
        start = pl.multiple_of(n * MOBA_BLOCK, MOBA_BLOCK)
        k_n = kb_ref[pl.ds(start, MOBA_BLOCK), :]
        vt_n = vt_ref[:, pl.ds(start, MOBA_BLOCK)]
        tile = jnp.minimum(own - n, BIAS_TILES - 1)
        for e in range(2):
            st = _mm_nt(k_n, qs_ref[t, e]) + bias_ref[e, tile]
            st = jnp.where(sel_ref[t, e, pl.ds(n, 1), :] > 0.0, st, NEG)
            m_old = m_ref[t, e]
            m_new = jnp.maximum(m_old, jnp.max(st, axis=0, keepdims=True))
            alpha = jnp.exp2(m_old - m_new)
            p = jnp.exp2(st - m_new)
            m_ref[t, e] = m_new
            l_ref[t, e] = alpha * l_ref[t, e] + jnp.sum(p, axis=0, keepdims=True)
            acc_ref[t, e] = alpha * acc_ref[t, e] + jnp.dot(vt_n, p.astype(MXU_DTYPE),
                                                            preferred_element_type=jnp.float32)

    for t in range(2):
        o0 = acc_ref[t, 0] / l_ref[t, 0]
        o1 = acc_ref[t, 1] / l_ref[t, 1]
        ot = jnp.concatenate([o0[:MOBA_HEAD_DIM], o1[MOBA_HEAD_DIM:]], axis=0)
        row0 = pl.multiple_of(owns[t] * tq, tq)
        o_ref[0, pl.ds(row0, tq), :] = ot.T


def _attention(q, k, v, bias_tiles):
    B, S, aw = q.shape
    tq = ATT_TQ
    assert S % (2 * MOBA_BLOCK) == 0 and tq == MOBA_BLOCK
    n_blk = S // MOBA_BLOCK
    n_pair = aw // LANES
    return pl.pallas_call(
        _attn_kernel,
        grid=(n_pair, B, n_blk // 2),
        in_specs=[
            pl.BlockSpec((1, tq, LANES), lambda hp, b, j: (b, j, hp)),
            pl.BlockSpec((1, tq, LANES), lambda hp, b, j: (b, n_blk - 1 - j, hp)),
            pl.BlockSpec((1, S, LANES), lambda hp, b, j: (b, 0, hp)),
            pl.BlockSpec((1, S, LANES), lambda hp, b, j: (b, 0, hp)),
            pl.BlockSpec((2, BIAS_TILES, MOBA_BLOCK, MOBA_BLOCK), lambda hp, b, j: (hp, 0, 0, 0)),
        ],
        out_specs=pl.BlockSpec((1, S, LANES), lambda hp, b, j: (b, 0, hp)),
        out_shape=jax.ShapeDtypeStruct((B, S, aw), jnp.float32),
        scratch_shapes=[
            pltpu.VMEM((S, LANES), MXU_DTYPE),
            pltpu.VMEM((LANES, S), MXU_DTYPE),
            pltpu.VMEM((n_blk, LANES), jnp.float32),
            pltpu.VMEM((2, 2, tq, LANES), MXU_DTYPE),
            pltpu.VMEM((2, 2, n_blk, tq), jnp.float32),
            pltpu.VMEM((2, 2, 1, tq), jnp.float32),
            pltpu.VMEM((2, 2, 1, tq), jnp.float32),
            pltpu.VMEM((2, 2, LANES, tq), jnp.float32),
        ],
        compiler_params=pltpu.CompilerParams(dimension_semantics=("arbitrary", "arbitrary", "arbitrary"),
                                             vmem_limit_bytes=VMEM_LIMIT),
        name="moba_attention",
    )(q, q, k, v, bias_tiles)


def kernel(x, ev_w_in, ev_conv_w, ev_conv_b, ev_conv_ln_g, ev_conv_ln_b, ev_sgu_ln_g, ev_sgu_ln_b, ev_sgu_w, ev_sgu_b, ev_w_out, od_w_qkv, od_w_o, rel_bias, peer_w_q, peer_subkeys, peer_u, peer_v, norm_mix_g, norm_ffn_g, norm_final_g):
    B, S, D = x.shape
    T = B * S
    x = _mixer(x, norm_mix_g[0], ev_w_in[0], ev_conv_w[0], ev_conv_b[0], ev_conv_ln_g[0], ev_conv_ln_b[0],
               ev_sgu_ln_g[0], ev_sgu_ln_b[0], ev_sgu_w[0], ev_sgu_b[0], ev_w_out[0])
    xt = _peer(x.reshape(T, D), norm_ffn_g[0], peer_w_q[0], peer_subkeys[0], peer_u[0], peer_v[0],
               norm_final_g, final_norm=False)
    q, k, v = _qkv(xt, norm_mix_g[1], od_w_qkv[0])
    aw = MOBA_HEADS * MOBA_HEAD_DIM
    o = _attention(q.reshape(B, S, aw), k.reshape(B, S, aw), v.reshape(B, S, aw), _bias_tiles(rel_bias))
    xt = _oproj(xt, o.reshape(T, aw), od_w_o[0])
    xt = _peer(xt, norm_ffn_g[1], peer_w_q[1], peer_subkeys[1], peer_u[1], peer_v[1],
               norm_final_g, final_norm=True)
    return xt.reshape(B, S, D)
```

```python
import functools
import math

import jax
import jax.numpy as jnp
from jax import lax
from jax.experimental import pallas as pl
from jax.experimental.pallas import tpu as pltpu

D_MODEL = 1024
A_CH = 512
CONV_W = 31
B_HEADS = 8
B_HEAD_DIM = 64
B_CH = B_HEADS * B_HEAD_DIM
SGU_CHUNK = 128
MIX_IN = 2 * A_CH + 2 * B_CH
MOBA_HEADS = 16
MOBA_HEAD_DIM = 64
MOBA_BLOCK = 256
MOBA_TOPK = 3
REL_BUCKETS = 32
REL_MAX_DIST = 2048
PEER_HEADS = 8
PEER_NKEYS = 128
PEER_N_EXPERTS = PEER_NKEYS * PEER_NKEYS
PEER_DKEY = 256
PEER_TOPK = 16
EPS = 1e-6
NEG = -1e30
LOG2E = math.log2(math.e)

LANES = 128
MXU_DTYPE = jnp.bfloat16
ROUTE_DTYPE = jnp.bfloat16
VMEM_LIMIT = 56 * 1024 * 1024

MIX_TS = 512
CONV_HALO = 32
PEER_TB = 512
PEER_EC = 1024
MM_TM = 512
ATT_TQ = MOBA_BLOCK
BIAS_TILES = 8


def _rms(x, g):
    return x * lax.rsqrt(jnp.mean(x * x, axis=-1, keepdims=True) + EPS) * g


def _layer_norm(x, g, b):
    mu = jnp.mean(x, axis=-1, keepdims=True)
    var = jnp.mean(jnp.square(x - mu), axis=-1, keepdims=True)
    return (x - mu) * lax.rsqrt(var + EPS) * g + b


def _gelu(x):
    cdf = 0.5 * (1.0 + jnp.tanh(math.sqrt(2.0 / math.pi) * (x + 0.044715 * (x * x * x))))
    return x * cdf


def _mm(a, b):
    return jnp.dot(a.astype(MXU_DTYPE), b.astype(MXU_DTYPE), preferred_element_type=jnp.float32)


def _mm_nt(a, b):
    return lax.dot_general(a.astype(MXU_DTYPE), b.astype(MXU_DTYPE), (((1,), (1,)), ((), ())),
                           preferred_element_type=jnp.float32)


def _mixer_kernel(x_ref, g_ref, win_ref, cw_ref, cb_ref, clg_ref, clb_ref, slg_ref, slb_ref,
                  sw_ref, sb_ref, wout_ref, o_ref, abuf_ref, s_ref):
    ts = x_ref.shape[1]
    x = x_ref[0]
    h = _rms(x, g_ref[...])
    z = _mm(h, win_ref[...])
    a = z[:, :A_CH] * jax.nn.sigmoid(z[:, A_CH:2 * A_CH])

    @pl.when(pl.program_id(1) == 0)
    def _():
        abuf_ref[0:CONV_HALO, :] = jnp.zeros((CONV_HALO, A_CH), jnp.float32)

    @pl.when(pl.program_id(1) > 0)
    def _():
        abuf_ref[0:CONV_HALO, :] = abuf_ref[ts:ts + CONV_HALO, :]

    abuf_ref[CONV_HALO:CONV_HALO + ts, :] = a
    acc = jnp.broadcast_to(cb_ref[...], (ts, A_CH))
    for w in range(CONV_W):
        off = CONV_HALO - (CONV_W - 1) + w
        acc = acc + cw_ref[w:w + 1, :] * abuf_ref[off:off + ts, :]
    a_n = _layer_norm(acc, clg_ref[...], clb_ref[...])
    a_out = a_n * jax.nn.sigmoid(a_n)

    u = _gelu(z[:, 2 * A_CH:2 * A_CH + B_CH])
    v = _layer_norm(_gelu(z[:, 2 * A_CH + B_CH:]), slg_ref[...], slb_ref[...])
    row = lax.broadcasted_iota(jnp.int32, (SGU_CHUNK, SGU_CHUNK), 0)
    col = lax.broadcasted_iota(jnp.int32, (SGU_CHUNK, SGU_CHUNK), 1)
    low_head = lax.broadcasted_iota(jnp.int32, (SGU_CHUNK, LANES), 1) < B_HEAD_DIM
    wm = [jnp.where(col <= row, sw_ref[hh], 0.0).astype(MXU_DTYPE) for hh in range(B_HEADS)]
    for c in range(ts // SGU_CHUNK):
        for pr in range(B_HEADS // 2):
            vp = v[c * SGU_CHUNK:(c + 1) * SGU_CHUNK, pr * LANES:(pr + 1) * LANES].astype(MXU_DTYPE)
            s0 = jnp.dot(wm[2 * pr], vp, preferred_element_type=jnp.float32)
            s1 = jnp.dot(wm[2 * pr + 1], vp, preferred_element_type=jnp.float32)
            s_ref[c * SGU_CHUNK:(c + 1) * SGU_CHUNK, pr * LANES:(pr + 1) * LANES] = (
                jnp.where(low_head, s0, s1) + sb_ref[:, pr * LANES:(pr + 1) * LANES])
    bo = u * s_ref[...]
    out = _mm(a_out, wout_ref[0:A_CH, :]) + _mm(bo, wout_ref[A_CH:, :])
    o_ref[0] = x + out


def _mixer(x, g, w_in, conv_w, conv_b, cln_g, cln_b, sln_g, sln_b, sgu_w, sgu_b, w_out):
    B, S, D = x.shape
    ts = min(MIX_TS, S)
    assert S % ts == 0 and ts % SGU_CHUNK == 0
    row2 = lambda a: a.reshape(1, -1)
    sb_exp = jnp.repeat(sgu_b.T, B_HEAD_DIM, axis=1)
    const = lambda shape: pl.BlockSpec(shape, lambda b, s: (0,) * len(shape))
    return pl.pallas_call(
        _mixer_kernel,
        grid=(B, S // ts),
        in_specs=[
            pl.BlockSpec((1, ts, D), lambda b, s: (b, s, 0)),
            const((1, D)), const((D, MIX_IN)), const((CONV_W, A_CH)), const((1, A_CH)),
            const((1, A_CH)), const((1, A_CH)), const((1, B_CH)), const((1, B_CH)),
            const((B_HEADS, SGU_CHUNK, SGU_CHUNK)), const((SGU_CHUNK, B_CH)), const((A_CH + B_CH, D)),
        ],
        out_specs=pl.BlockSpec((1, ts, D), lambda b, s: (b, s, 0)),
        out_shape=jax.ShapeDtypeStruct((B, S, D), jnp.float32),
        scratch_shapes=[pltpu.VMEM((ts + CONV_HALO, A_CH), jnp.float32),
                        pltpu.VMEM((ts, B_CH), jnp.float32)],
        compiler_params=pltpu.CompilerParams(dimension_semantics=("arbitrary", "arbitrary"),
                                             vmem_limit_bytes=VMEM_LIMIT),
        name="mixer0",
    )(x, row2(g), w_in.astype(MXU_DTYPE), conv_w, row2(conv_b), row2(cln_g), row2(cln_b),
      row2(sln_g), row2(sln_b), sgu_w, sb_exp, w_out.astype(MXU_DTYPE))


def _candidate_pairs():
    return [(a, b) for a in range(PEER_TOPK) for b in range(PEER_TOPK) if (a + 1) * (b + 1) <= PEER_TOPK]


def _top16(s, iota_f, vals_ref, p, hh):
    work = s
    rank = jnp.full(s.shape, float(PEER_TOPK), jnp.float32)
    for r in range(PEER_TOPK):
        m = jnp.max(work, axis=0, keepdims=True)
        first = jnp.min(jnp.where(work == m, iota_f, float(PEER_NKEYS)), axis=0, keepdims=True)
        sel = iota_f == first
        rank = jnp.where(sel, float(r), rank)
        work = jnp.where(sel, -jnp.inf, work)
        vals_ref[p, r, pl.ds(hh, 1), :] = m
    return rank


def _gelu_sigmoid_form(x):
    k0 = -2.0 * math.sqrt(2.0 / math.pi) * math.log2(math.e)
    u = x * (k0 + (k0 * 0.044715) * (x * x))
    return x / (1.0 + jnp.exp2(u))


TOP16_MARK = 2.0 ** 127


def _top16_no_ties(s, vals_ref, p, hh):
    work = s
    for r in range(PEER_TOPK):
        m = jnp.max(work, axis=0, keepdims=True)
        work = jnp.where(work == m, -(1.0 + (PEER_TOPK - 1 - r) / PEER_TOPK) * TOP16_MARK, work)
        vals_ref[p, r, pl.ds(hh, 1), :] = m
    marked = work <= -TOP16_MARK
    rank = jnp.where(marked, (2.0 * PEER_TOPK - 1.0) + work * (PEER_TOPK / TOP16_MARK), float(PEER_TOPK))
    n_marked = jnp.sum(jnp.where(marked, 1.0, 0.0), axis=0, keepdims=True)
    ok = jnp.max(jnp.abs(n_marked - float(PEER_TOPK))) == 0.0
    return rank, ok


def _store_lane_tiles(ref, hh, x):
    for lt in range(x.shape[1] // LANES):
        ref[hh, lt] = x[:, lt * LANES:(lt + 1) * LANES]


def _bf16_pair_words(x):
    hi = pltpu.bitcast(x.astype(jnp.bfloat16).astype(jnp.float32), jnp.int32)
    return hi | lax.shift_right_logical(hi, jnp.full(hi.shape, 16, jnp.int32))


def _routed_tiles(a_ref, p_ref, step, rank2_ref, e2_ref, n_ref, e1_ref):
    ec, tb = a_ref.shape
    n_g = ec // PEER_NKEYS
    n_lt = tb // LANES
    rows_per_tile = 16

    def body(it, carry):
        g = it // n_lt
        lt = it % n_lt
        i = step * n_g + g
        lane0 = pl.multiple_of(lt * LANES, LANES)
        row0 = pl.multiple_of(g * PEER_NKEYS, PEER_NKEYS)
        n_b = [pltpu.bitcast(jnp.broadcast_to(n_ref[hh, lt, pl.ds(i, 1), :], (8, LANES)), ROUTE_DTYPE)
               for hh in range(PEER_HEADS)]
        e1_b = [pltpu.bitcast(jnp.broadcast_to(e1_ref[hh, lt, pl.ds(i, 1), :], (8, LANES)), ROUTE_DTYPE)
                for hh in range(PEER_HEADS)]
        for sg in range(PEER_NKEYS // rows_per_tile):
            rows = slice(sg * rows_per_tile, (sg + 1) * rows_per_tile)
            wgt = None
            for hh in range(PEER_HEADS):
                term = jnp.where(rank2_ref[hh, lt, rows, :] < n_b[hh], e2_ref[hh, lt, rows, :],
                                 jnp.zeros((), ROUTE_DTYPE)) * e1_b[hh]
                wgt = term if wgt is None else wgt + term
            arows = pl.ds(row0 + sg * rows_per_tile, rows_per_tile)
            act = _gelu_sigmoid_form(a_ref[arows, pl.ds(lane0, LANES)])
            p_ref[arows, pl.ds(lane0, LANES)] = (wgt * act.astype(ROUTE_DTYPE)).astype(p_ref.dtype)
        return carry

    lax.fori_loop(0, n_g * n_lt, body, 0, unroll=2)


def _peer_kernel(x_ref, g_ref, wq_ref, sk_ref, u_ref, vt_ref, gf_ref, o_ref,
                 ht_ref, yt_ref, q_ref, rank1_ref, rank2_ref, e1_ref, e2_ref, n_ref, e1w_ref,
                 vals_ref, na_ref, a_ref, p_ref, *, final_norm):
    k = pl.program_id(1)
    tb = x_ref.shape[0]

    @pl.when(k == 0)
    def _route():
        h = _rms(x_ref[...], g_ref[...])
        ht = h.T.astype(MXU_DTYPE)
        ht_ref[...] = ht
        q_ref[...] = jnp.dot(wq_ref[...], ht, preferred_element_type=jnp.float32)
        yt_ref[...] = jnp.zeros_like(yt_ref)
        iota_f = lax.broadcasted_iota(jnp.int32, (PEER_NKEYS, tb), 0).astype(jnp.float32)

        def head_scores(hh, carry):
            scores = []
            all_ok = None
            for p in range(2):
                idx = hh * 2 + p
                start = pl.multiple_of(idx * PEER_NKEYS, PEER_NKEYS)
                qhp = q_ref[pl.ds(start, PEER_NKEYS), :]
                s = jnp.dot(sk_ref[idx], qhp.astype(MXU_DTYPE), preferred_element_type=jnp.float32)
                s = s + 0.0
                scores.append(s)
                rank, ok = _top16_no_ties(s, vals_ref, p, hh)
                all_ok = ok if all_ok is None else jnp.logical_and(all_ok, ok)
                e = jnp.exp(s - vals_ref[p, 0, pl.ds(hh, 1), :])
                if p == 0:
                    rank1_ref[hh] = rank
                    e1_ref[hh] = e
                else:
                    _store_lane_tiles(rank2_ref, hh, rank.astype(ROUTE_DTYPE))
                    _store_lane_tiles(e2_ref, hh, e.astype(ROUTE_DTYPE))

            @pl.when(jnp.logical_not(all_ok))
            def _exact():
                rank1_ref[hh] = _top16(scores[0], iota_f, vals_ref, 0, hh)
                _store_lane_tiles(rank2_ref, hh, _top16(scores[1], iota_f, vals_ref, 1, hh).astype(ROUTE_DTYPE))
            return carry

        lax.fori_loop(0, PEER_HEADS, head_scores, 0)

        pairs = _candidate_pairs()
        v1 = [vals_ref[0, a] for a in range(PEER_TOPK)]
        v2 = [vals_ref[1, b] for b in range(PEER_TOPK)]
        cand = {ab: v1[ab[0]] + v2[ab[1]] for ab in pairs}
        pos = {ab: jnp.full((PEER_HEADS, tb), float((ab[0] + 1) * (ab[1] + 1) - 1), jnp.float32) for ab in pairs}
        for i, x_ab in enumerate(pairs):
            for y_ab in pairs[i + 1:]:
                (a, b), (a2, b2) = x_ab, y_ab
                if (a2 >= a and b2 >= b) or (a2 <= a and b2 <= b):
                    continue
                x_first = cand[x_ab] >= cand[y_ab]
                pos[y_ab] = pos[y_ab] + jnp.where(x_first, 1.0, 0.0)
                pos[x_ab] = pos[x_ab] + jnp.where(x_first, 0.0, 1.0)
        c00 = cand[(0, 0)]
        z = jnp.zeros((PEER_HEADS, tb), jnp.float32)
        n_a = [jnp.zeros((PEER_HEADS, tb), jnp.float32) for _ in range(PEER_TOPK)]
        for (a, b) in pairs:
            chosen = pos[(a, b)] < float(PEER_TOPK)
            n_a[a] = n_a[a] + jnp.where(chosen, 1.0, 0.0)
            z = z + jnp.where(chosen, jnp.exp(cand[(a, b)] - c00), 0.0)
        inv_z = 1.0 / z
        for a in range(PEER_TOPK):
            na_ref[a] = n_a[a]
        na_ref[PEER_TOPK] = inv_z

        def head_rows(hh, carry):
            rank1 = rank1_ref[hh]
            n_i = jnp.zeros((PEER_NKEYS, tb), jnp.float32)
            for a in range(PEER_TOPK):
                n_i = jnp.where(rank1 == float(a), na_ref[a, pl.ds(hh, 1), :], n_i)
            _store_lane_tiles(n_ref, hh, _bf16_pair_words(n_i))
            _store_lane_tiles(e1w_ref, hh, _bf16_pair_words(e1_ref[hh] * na_ref[PEER_TOPK, pl.ds(hh, 1), :]))
            return carry

        lax.fori_loop(0, PEER_HEADS, head_rows, 0)

    a_ref[...] = jnp.dot(u_ref[...], ht_ref[...], preferred_element_type=jnp.float32)
    _routed_tiles(a_ref, p_ref, k, rank2_ref, e2_ref, n_ref, e1w_ref)
    yt_ref[...] += jnp.dot(vt_ref[...], p_ref[...], preferred_element_type=jnp.float32)

    @pl.when(k == pl.num_programs(1) - 1)
    def _finish():
        out = x_ref[...] + yt_ref[...].T
        if final_norm:
            out = _rms(out, gf_ref[...])
        o_ref[...] = out


def _peer(x, g, w_q, subkeys, u_tab, v_tab, g_final, final_norm):
    T, D = x.shape
    tb = min(PEER_TB, T)
    ec = PEER_EC
    assert T % tb == 0 and tb % LANES == 0 and PEER_N_EXPERTS % ec == 0 and ec % PEER_NKEYS == 0
    n_chunk = PEER_N_EXPERTS // ec
    hq = PEER_HEADS * PEER_DKEY
    wq_t = w_q.T.astype(MXU_DTYPE)
    sk = subkeys.reshape(PEER_HEADS * 2, PEER_NKEYS, PEER_DKEY // 2).astype(MXU_DTYPE)
    u_b = u_tab.astype(MXU_DTYPE)
    vt_b = v_tab.T.astype(MXU_DTYPE)
    head_tile = (PEER_HEADS, PEER_NKEYS, tb)
    head_slabs = (PEER_HEADS, tb // LANES, PEER_NKEYS, LANES)
    return pl.pallas_call(
        functools.partial(_peer_kernel, final_norm=final_norm),
        grid=(T // tb, n_chunk),
        in_specs=[
            pl.BlockSpec((tb, D), lambda t, k: (t, 0)),
            pl.BlockSpec((1, D), lambda t, k: (0, 0)),
            pl.BlockSpec((hq, D), lambda t, k: (0, 0)),
            pl.BlockSpec((PEER_HEADS * 2, PEER_NKEYS, PEER_DKEY // 2), lambda t, k: (0, 0, 0)),
            pl.BlockSpec((ec, D), lambda t, k: (k, 0)),
            pl.BlockSpec((D, ec), lambda t, k: (0, k)),
            pl.BlockSpec((1, D), lambda t, k: (0, 0)),
        ],
        out_specs=pl.BlockSpec((tb, D), lambda t, k: (t, 0)),
        out_shape=jax.ShapeDtypeStruct((T, D), jnp.float32),
        scratch_shapes=[
            pltpu.VMEM((D, tb), MXU_DTYPE),
            pltpu.VMEM((D, tb), jnp.float32),
            pltpu.VMEM((hq, tb), jnp.float32),
            pltpu.VMEM(head_tile, jnp.float32),
            pltpu.VMEM(head_slabs, ROUTE_DTYPE),
            pltpu.VMEM(head_tile, jnp.float32),
            pltpu.VMEM(head_slabs, ROUTE_DTYPE),
            pltpu.VMEM(head_slabs, jnp.int32),
            pltpu.VMEM(head_slabs, jnp.int32),
            pltpu.VMEM((2, PEER_TOPK, PEER_HEADS, tb), jnp.float32),
            pltpu.VMEM((PEER_TOPK + 1, PEER_HEADS, tb), jnp.float32),
            pltpu.VMEM((ec, tb), jnp.float32),
            pltpu.VMEM((ec, tb), MXU_DTYPE),
        ],
        compiler_params=pltpu.CompilerParams(dimension_semantics=("arbitrary", "arbitrary"),
                                             vmem_limit_bytes=VMEM_LIMIT),
        name="peer_final" if final_norm else "peer",
    )(x, g.reshape(1, D), wq_t, sk, u_b, vt_b, g_final.reshape(1, D))


def _qkv_kernel(x_ref, g_ref, w_ref, q_ref, k_ref, v_ref):
    h = _rms(x_ref[...], g_ref[...])
    qkv = _mm(h, w_ref[...])
    d = q_ref.shape[1]
    q_ref[...] = qkv[:, :d]
    k_ref[...] = qkv[:, d:2 * d]
    v_ref[...] = qkv[:, 2 * d:]


def _qkv(x, g, w_qkv):
    T, D = x.shape
    tm = min(MM_TM, T)
    aw = MOBA_HEADS * MOBA_HEAD_DIM
    out = jax.ShapeDtypeStruct((T, aw), jnp.float32)
    return pl.pallas_call(
        _qkv_kernel,
        grid=(T // tm,),
        in_specs=[pl.BlockSpec((tm, D), lambda i: (i, 0)),
                  pl.BlockSpec((1, D), lambda i: (0, 0)),
                  pl.BlockSpec((D, 3 * aw), lambda i: (0, 0))],
        out_specs=[pl.BlockSpec((tm, aw), lambda i: (i, 0))] * 3,
        out_shape=[out, out, out],
        compiler_params=pltpu.CompilerParams(dimension_semantics=("arbitrary",),
                                             vmem_limit_bytes=VMEM_LIMIT),
        name="qkv",
    )(x, g.reshape(1, D), w_qkv.astype(MXU_DTYPE))


def _oproj_kernel(x_ref, o_ref, w_ref, y_ref):
    y_ref[...] = x_ref[...] + _mm(o_ref[...], w_ref[...])


def _oproj(x, o, w_o):
    T, D = x.shape
    tm = min(MM_TM, T)
    aw = o.shape[1]
    return pl.pallas_call(
        _oproj_kernel,
        grid=(T // tm,),
        in_specs=[pl.BlockSpec((tm, D), lambda i: (i, 0)),
                  pl.BlockSpec((tm, aw), lambda i: (i, 0)),
                  pl.BlockSpec((aw, D), lambda i: (0, 0))],
        out_specs=pl.BlockSpec((tm, D), lambda i: (i, 0)),
        out_shape=jax.ShapeDtypeStruct((T, D), jnp.float32),
        compiler_params=pltpu.CompilerParams(dimension_semantics=("arbitrary",),
                                             vmem_limit_bytes=VMEM_LIMIT),
        name="oproj",
    )(x, o, w_o.astype(MXU_DTYPE))


def _t5_bucket(dist):
    n = jnp.maximum(dist, 0)
    max_exact = REL_BUCKETS // 2
    nf = jnp.maximum(n, 1).astype(jnp.float32)
    log_b = max_exact + (jnp.log(nf / max_exact) / math.log(REL_MAX_DIST / max_exact)
                         * (REL_BUCKETS - max_exact)).astype(jnp.int32)
    return jnp.where(n < max_exact, n, jnp.minimum(log_b, REL_BUCKETS - 1))


def _bias_kernel(rb_ref, o_ref):
    h = pl.program_id(0)
    dblk = pl.program_id(1)
    key = lax.broadcasted_iota(jnp.int32, (MOBA_BLOCK, MOBA_BLOCK), 0)
    qry = lax.broadcasted_iota(jnp.int32, (MOBA_BLOCK, MOBA_BLOCK), 1)
    dist = dblk * MOBA_BLOCK + qry - key
    bucket = _t5_bucket(dist)
    val = jnp.zeros((MOBA_BLOCK, MOBA_BLOCK), jnp.float32)
    for b in range(REL_BUCKETS):
        val = jnp.where(bucket == b, rb_ref[b, h], val)
    o_ref[0, 0] = jnp.where(dist >= 0, val * LOG2E, NEG)


def _bias_tiles(rel_bias):
    last_tile_min_dist = (BIAS_TILES - 1) * MOBA_BLOCK - (MOBA_BLOCK - 1)
    max_exact = REL_BUCKETS // 2
    first_last_bucket = max_exact * (REL_MAX_DIST / max_exact) ** ((REL_BUCKETS - 1 - max_exact) / (REL_BUCKETS - max_exact))
    assert last_tile_min_dist > first_last_bucket + 1
    return pl.pallas_call(
        _bias_kernel,
        grid=(MOBA_HEADS, BIAS_TILES),
        in_specs=[pl.BlockSpec(memory_space=pltpu.SMEM)],
        out_specs=pl.BlockSpec((1, 1, MOBA_BLOCK, MOBA_BLOCK), lambda h, d: (h, d, 0, 0)),
        out_shape=jax.ShapeDtypeStruct((MOBA_HEADS, BIAS_TILES, MOBA_BLOCK, MOBA_BLOCK), jnp.float32),
        compiler_params=pltpu.CompilerParams(dimension_semantics=("arbitrary", "arbitrary")),
        name="t5_bias_tiles",
    )(rel_bias)


def _attn_kernel(qa_ref, qb_ref, k_ref, v_ref, bias_ref, o_ref,
                 kb_ref, vt_ref, kmean_ref, qs_ref, sel_ref, m_ref, l_ref, acc_ref):
    j = pl.program_id(2)
    S = k_ref.shape[1]
    n_blk = S // MOBA_BLOCK
    tq = qa_ref.shape[1]

    @pl.when(j == 0)
    def _prep():
        for n in range(n_blk):
            kblk = k_ref[0, n * MOBA_BLOCK:(n + 1) * MOBA_BLOCK, :]
            kb_ref[n * MOBA_BLOCK:(n + 1) * MOBA_BLOCK, :] = kblk.astype(MXU_DTYPE)
            kmean_ref[n:n + 1, :] = jnp.mean(kblk, axis=0, keepdims=True)
            vt_ref[:, n * MOBA_BLOCK:(n + 1) * MOBA_BLOCK] = (
                v_ref[0, n * MOBA_BLOCK:(n + 1) * MOBA_BLOCK, :].T.astype(MXU_DTYPE))

    owns = (j, n_blk - 1 - j)
    low_head = lax.broadcasted_iota(jnp.int32, (tq, LANES), 1) < MOBA_HEAD_DIM
    blk = lax.broadcasted_iota(jnp.int32, (n_blk, tq), 0)
    for t, q_ref in enumerate((qa_ref, qb_ref)):
        q = q_ref[0]
        past = blk < owns[t]
        for e in range(2):
            q_e = jnp.where(low_head, q, 0.0) if e == 0 else jnp.where(low_head, 0.0, q)
            gate = jnp.where(past, _mm_nt(kmean_ref[...], q_e), -jnp.inf)
            beaten = jnp.zeros((n_blk, tq), jnp.float32)
            for m in range(n_blk):
                gm = gate[m:m + 1, :]
                beats = (gm > gate) | ((gm == gate) & (blk > m))
                beaten = beaten + jnp.where(beats, 1.0, 0.0)
            chosen = (past & (beaten < float(MOBA_TOPK))) | (blk == owns[t])
            sel_ref[t, e] = jnp.where(chosen, 1.0, 0.0)
            qs_ref[t, e] = (q_e * (MOBA_HEAD_DIM ** -0.5 * LOG2E)).astype(MXU_DTYPE)
            m_ref[t, e] = jnp.full((1, tq), NEG, jnp.float32)
            l_ref[t, e] = jnp.zeros((1, tq), jnp.float32)
            acc_ref[t, e] = jnp.zeros((LANES, tq), jnp.float32)

    for u in range(n_blk + 1):
        is_a = u <= j
        t = jnp.where(is_a, 0, 1)
        n = jnp.where(is_a, j - u, n_blk - u)
        own = jnp.where(is_a, j, n_blk - 1 - j)
        start = pl.multiple_of(n * MOBA_BLOCK, MOBA_BLOCK)
        k_n = kb_ref[pl.ds(start, MOBA_BLOCK), :]
        vt_n = vt_ref[:, pl.ds(start, MOBA_BLOCK)]
        tile = jnp.minimum(own - n, BIAS_TILES - 1)
        for e in range(2):
            st = _mm_nt(k_n, qs_ref[t, e]) + bias_ref[e, tile]
            st = jnp.where(sel_ref[t, e, pl.ds(n, 1), :] > 0.0, st, NEG)
            m_old = m_ref[t, e]
            m_new = jnp.maximum(m_old, jnp.max(st, axis=0, keepdims=True))
            alpha = jnp.exp2(m_old - m_new)
            p = jnp.exp2(st - m_new)
            m_ref[t, e] = m_new
            l_ref[t, e] = alpha * l_ref[t, e] + jnp.sum(p, axis=0, keepdims=True)
            acc_ref[t, e] = alpha * acc_ref[t, e] + jnp.dot(vt_n, p.astype(MXU_DTYPE),
                                                            preferred_element_type=jnp.float32)

    for t in range(2):
        o0 = acc_ref[t, 0] / l_ref[t, 0]
        o1 = acc_ref[t, 1] / l_ref[t, 1]
        ot = jnp.concatenate([o0[:MOBA_HEAD_DIM], o1[MOBA_HEAD_DIM:]], axis=0)
        row0 = pl.multiple_of(owns[t] * tq, tq)
        o_ref[0, pl.ds(row0, tq), :] = ot.T


def _attention(q, k, v, bias_tiles):
    B, S, aw = q.shape
    tq = ATT_TQ
    assert S % (2 * MOBA_BLOCK) == 0 and tq == MOBA_BLOCK
    n_blk = S // MOBA_BLOCK
    n_pair = aw // LANES
    return pl.pallas_call(
        _attn_kernel,
        grid=(n_pair, B, n_blk // 2),
        in_specs=[
            pl.BlockSpec((1, tq, LANES), lambda hp, b, j: (b, j, hp)),
            pl.BlockSpec((1, tq, LANES), lambda hp, b, j: (b, n_blk - 1 - j, hp)),
            pl.BlockSpec((1, S, LANES), lambda hp, b, j: (b, 0, hp)),
            pl.BlockSpec((1, S, LANES), lambda hp, b, j: (b, 0, hp)),
            pl.BlockSpec((2, BIAS_TILES, MOBA_BLOCK, MOBA_BLOCK), lambda hp, b, j: (hp, 0, 0, 0)),
        ],
        out_specs=pl.BlockSpec((1, S, LANES), lambda hp, b, j: (b, 0, hp)),
        out_shape=jax.ShapeDtypeStruct((B, S, aw), jnp.float32),
        scratch_shapes=[
            pltpu.VMEM((S, LANES), MXU_DTYPE),
            pltpu.VMEM((LANES, S), MXU_DTYPE),
            pltpu.VMEM((n_blk, LANES), jnp.float32),
            pltpu.VMEM((2, 2, tq, LANES), MXU_DTYPE),
            pltpu.VMEM((2, 2, n_blk, tq), jnp.float32),
            pltpu.VMEM((2, 2, 1, tq), jnp.float32),
            pltpu.VMEM((2, 2, 1, tq), jnp.float32),
            pltpu.VMEM((2, 2, LANES, tq), jnp.float32),
        ],
        compiler_params=pltpu.CompilerParams(dimension_semantics=("arbitrary", "arbitrary", "arbitrary"),
                                             vmem_limit_bytes=VMEM_LIMIT),
        name="moba_attention",
    )(q, q, k, v, bias_tiles)


def kernel(x, ev_w_in, ev_conv_w, ev_conv_b, ev_conv_ln_g, ev_conv_ln_b, ev_sgu_ln_g, ev_sgu_ln_b, ev_sgu_w, ev_sgu_b, ev_w_out, od_w_qkv, od_w_o, rel_bias, peer_w_q, peer_subkeys, peer_u, peer_v, norm_mix_g, norm_ffn_g, norm_final_g):
    B, S, D = x.shape
    T = B * S
    x = _mixer(x, norm_mix_g[0], ev_w_in[0], ev_conv_w[0], ev_conv_b[0], ev_conv_ln_g[0], ev_conv_ln_b[0],
               ev_sgu_ln_g[0], ev_sgu_ln_b[0], ev_sgu_w[0], ev_sgu_b[0], ev_w_out[0])
    xt = _peer(x.reshape(T, D), norm_ffn_g[0], peer_w_q[0], peer_subkeys[0], peer_u[0], peer_v[0],
               norm_final_g, final_norm=False)
    q, k, v = _qkv(xt, norm_mix_g[1], od_w_qkv[0])
    aw = MOBA_HEADS * MOBA_HEAD_DIM
    o = _attention(q.reshape(B, S, aw), k.reshape(B, S, aw), v.reshape(B, S, aw), _bias_tiles(rel_bias))
    xt = _oproj(xt, o.reshape(T, aw), od_w_o[0])
    xt = _peer(xt, norm_ffn_g[1], peer_w_q[1], peer_subkeys[1], peer_u[1], peer_v[1],
               norm_final_g, final_norm=True)
    return xt.reshape(B, S, D)
```

```python
import functools
import math

import jax
import jax.numpy as jnp
from jax import lax
from jax.experimental import pallas as pl
from jax.experimental.pallas import tpu as pltpu

D_MODEL = 1024
A_CH = 512
CONV_W = 31
B_HEADS = 8
B_HEAD_DIM = 64
B_CH = B_HEADS * B_HEAD_DIM
SGU_CHUNK = 128
MIX_IN = 2 * A_CH + 2 * B_CH
MOBA_HEADS = 16
MOBA_HEAD_DIM = 64
MOBA_BLOCK = 256
MOBA_TOPK = 3
REL_BUCKETS = 32
REL_MAX_DIST = 2048
PEER_HEADS = 8
PEER_NKEYS = 128
PEER_N_EXPERTS = PEER_NKEYS * PEER_NKEYS
PEER_DKEY = 256
PEER_TOPK = 16
EPS = 1e-6
NEG = -1e30
LOG2E = math.log2(math.e)

LANES = 128
MXU_DTYPE = jnp.bfloat16
ROUTE_DTYPE = jnp.bfloat16
VMEM_LIMIT = 56 * 1024 * 1024

MIX_TS = 512
CONV_HALO = 32
PEER_TB = 512
PEER_EC = 512
MM_TM = 512
ATT_TQ = MOBA_BLOCK
BIAS_TILES = 8


def _rms(x, g):
    return x * lax.rsqrt(jnp.mean(x * x, axis=-1, keepdims=True) + EPS) * g


def _layer_norm(x, g, b):
    mu = jnp.mean(x, axis=-1, keepdims=True)
    var = jnp.mean(jnp.square(x - mu), axis=-1, keepdims=True)
    return (x - mu) * lax.rsqrt(var + EPS) * g + b


def _gelu(x):
    cdf = 0.5 * (1.0 + jnp.tanh(math.sqrt(2.0 / math.pi) * (x + 0.044715 * (x * x * x))))
    return x * cdf


def _mm(a, b):
    return jnp.dot(a.astype(MXU_DTYPE), b.astype(MXU_DTYPE), preferred_element_type=jnp.float32)


def _mm_nt(a, b):
    return lax.dot_general(a.astype(MXU_DTYPE), b.astype(MXU_DTYPE), (((1,), (1,)), ((), ())),
                           preferred_element_type=jnp.float32)


def _mixer_kernel(x_ref, g_ref, win_ref, cw_ref, cb_ref, clg_ref, clb_ref, slg_ref, slb_ref,
                  sw_ref, sb_ref, wout_ref, o_ref, abuf_ref, s_ref):
    ts = x_ref.shape[1]
    x = x_ref[0]
    h = _rms(x, g_ref[...])
    z = _mm(h, win_ref[...])
    a = z[:, :A_CH] * jax.nn.sigmoid(z[:, A_CH:2 * A_CH])

    @pl.when(pl.program_id(1) == 0)
    def _():
        abuf_ref[0:CONV_HALO, :] = jnp.zeros((CONV_HALO, A_CH), jnp.float32)

    @pl.when(pl.program_id(1) > 0)
    def _():
        abuf_ref[0:CONV_HALO, :] = abuf_ref[ts:ts + CONV_HALO, :]

    abuf_ref[CONV_HALO:CONV_HALO + ts, :] = a
    acc = jnp.broadcast_to(cb_ref[...], (ts, A_CH))
    for w in range(CONV_W):
        off = CONV_HALO - (CONV_W - 1) + w
        acc = acc + cw_ref[w:w + 1, :] * abuf_ref[off:off + ts, :]
    a_n = _layer_norm(acc, clg_ref[...], clb_ref[...])
    a_out = a_n * jax.nn.sigmoid(a_n)

    u = _gelu(z[:, 2 * A_CH:2 * A_CH + B_CH])
    v = _layer_norm(_gelu(z[:, 2 * A_CH + B_CH:]), slg_ref[...], slb_ref[...])
    row = lax.broadcasted_iota(jnp.int32, (SGU_CHUNK, SGU_CHUNK), 0)
    col = lax.broadcasted_iota(jnp.int32, (SGU_CHUNK, SGU_CHUNK), 1)
    low_head = lax.broadcasted_iota(jnp.int32, (SGU_CHUNK, LANES), 1) < B_HEAD_DIM
    wm = [jnp.where(col <= row, sw_ref[hh], 0.0).astype(MXU_DTYPE) for hh in range(B_HEADS)]
    for c in range(ts // SGU_CHUNK):
        for pr in range(B_HEADS // 2):
            vp = v[c * SGU_CHUNK:(c + 1) * SGU_CHUNK, pr * LANES:(pr + 1) * LANES].astype(MXU_DTYPE)
            s0 = jnp.dot(wm[2 * pr], vp, preferred_element_type=jnp.float32)
            s1 = jnp.dot(wm[2 * pr + 1], vp, preferred_element_type=jnp.float32)
            s_ref[c * SGU_CHUNK:(c + 1) * SGU_CHUNK, pr * LANES:(pr + 1) * LANES] = (
                jnp.where(low_head, s0, s1) + sb_ref[:, pr * LANES:(pr + 1) * LANES])
    bo = u * s_ref[...]
    out = _mm(a_out, wout_ref[0:A_CH, :]) + _mm(bo, wout_ref[A_CH:, :])
    o_ref[0] = x + out


def _mixer(x, g, w_in, conv_w, conv_b, cln_g, cln_b, sln_g, sln_b, sgu_w, sgu_b, w_out):
    B, S, D = x.shape
    ts = min(MIX_TS, S)
    assert S % ts == 0 and ts % SGU_CHUNK == 0
    row2 = lambda a: a.reshape(1, -1)
    sb_exp = jnp.repeat(sgu_b.T, B_HEAD_DIM, axis=1)
    const = lambda shape: pl.BlockSpec(shape, lambda b, s: (0,) * len(shape))
    return pl.pallas_call(
        _mixer_kernel,
        grid=(B, S // ts),
        in_specs=[
            pl.BlockSpec((1, ts, D), lambda b, s: (b, s, 0)),
            const((1, D)), const((D, MIX_IN)), const((CONV_W, A_CH)), const((1, A_CH)),
            const((1, A_CH)), const((1, A_CH)), const((1, B_CH)), const((1, B_CH)),
            const((B_HEADS, SGU_CHUNK, SGU_CHUNK)), const((SGU_CHUNK, B_CH)), const((A_CH + B_CH, D)),
        ],
        out_specs=pl.BlockSpec((1, ts, D), lambda b, s: (b, s, 0)),
        out_shape=jax.ShapeDtypeStruct((B, S, D), jnp.float32),
        scratch_shapes=[pltpu.VMEM((ts + CONV_HALO, A_CH), jnp.float32),
                        pltpu.VMEM((ts, B_CH), jnp.float32)],
        compiler_params=pltpu.CompilerParams(dimension_semantics=("arbitrary", "arbitrary"),
                                             vmem_limit_bytes=VMEM_LIMIT),
        name="mixer0",
    )(x, row2(g), w_in.astype(MXU_DTYPE), conv_w, row2(conv_b), row2(cln_g), row2(cln_b),
      row2(sln_g), row2(sln_b), sgu_w, sb_exp, w_out.astype(MXU_DTYPE))


def _candidate_pairs():
    return [(a, b) for a in range(PEER_TOPK) for b in range(PEER_TOPK) if (a + 1) * (b + 1) <= PEER_TOPK]


def _top16(s, iota_f, vals_ref, p, hh):
    work = s
    rank = jnp.full(s.shape, float(PEER_TOPK), jnp.float32)
    for r in range(PEER_TOPK):
        m = jnp.max(work, axis=0, keepdims=True)
        first = jnp.min(jnp.where(work == m, iota_f, float(PEER_NKEYS)), axis=0, keepdims=True)
        sel = iota_f == first
        rank = jnp.where(sel, float(r), rank)
        work = jnp.where(sel, -jnp.inf, work)
        vals_ref[p, r, pl.ds(hh, 1), :] = m
    return rank


def _gelu_sigmoid_form(x):
    k0 = -2.0 * math.sqrt(2.0 / math.pi) * math.log2(math.e)
    u = x * (k0 + (k0 * 0.044715) * (x * x))
    return x / (1.0 + jnp.exp2(u))


TOP16_MARK = 2.0 ** 127


def _top16_no_ties(s, vals_ref, p, hh):
    work = s
    for r in range(PEER_TOPK):
        m = jnp.max(work, axis=0, keepdims=True)
        work = jnp.where(work == m, -(1.0 + (PEER_TOPK - 1 - r) / PEER_TOPK) * TOP16_MARK, work)
        vals_ref[p, r, pl.ds(hh, 1), :] = m
    marked = work <= -TOP16_MARK
    rank = jnp.where(marked, (2.0 * PEER_TOPK - 1.0) + work * (PEER_TOPK / TOP16_MARK), float(PEER_TOPK))
    n_marked = jnp.sum(jnp.where(marked, 1.0, 0.0), axis=0, keepdims=True)
    ok = jnp.max(jnp.abs(n_marked - float(PEER_TOPK))) == 0.0
    return rank, ok


def _routed_chunk(a_ref, p_ref, chunk, rank2_ref, e2_ref, n_ref, e1_ref):
    ec, tb = a_ref.shape
    rows_per_tile = 16
    for g in range(ec // PEER_NKEYS):
        i = jnp.clip(chunk * (ec // PEER_NKEYS) + g, 0, PEER_NKEYS - 1)
        n_rows = [n_ref[hh, pl.ds(i, 1), :] for hh in range(PEER_HEADS)]
        e1_rows = [e1_ref[hh, pl.ds(i, 1), :] for hh in range(PEER_HEADS)]
        for lt in range(tb // LANES):
            cols = slice(lt * LANES, (lt + 1) * LANES)
            n_b = [jnp.broadcast_to(n_rows[hh][:, cols], (rows_per_tile, LANES)).astype(ROUTE_DTYPE)
                   for hh in range(PEER_HEADS)]
            e1_b = [jnp.broadcast_to(e1_rows[hh][:, cols], (rows_per_tile, LANES)).astype(ROUTE_DTYPE)
                    for hh in range(PEER_HEADS)]
            for sg in range(PEER_NKEYS // rows_per_tile):
                rows = slice(sg * rows_per_tile, (sg + 1) * rows_per_tile)
                wgt = None
                for hh in range(PEER_HEADS):
                    term = jnp.where(rank2_ref[hh, rows, cols] < n_b[hh], e2_ref[hh, rows, cols],
                                     jnp.zeros((), ROUTE_DTYPE)) * e1_b[hh]
                    wgt = term if wgt is None else wgt + term
                arows = slice(g * PEER_NKEYS + sg * rows_per_tile, g * PEER_NKEYS + (sg + 1) * rows_per_tile)
                act = _gelu_sigmoid_form(a_ref[arows, cols])
                p_ref[arows, cols] = (wgt * act.astype(ROUTE_DTYPE)).astype(p_ref.dtype)


def _peer_kernel(x_ref, g_ref, wq_ref, sk_ref, u_ref, vt_ref, gf_ref, o_ref,
                 ht_ref, yt_ref, q_ref, rank1_ref, rank2_ref, e1_ref, e2_ref, n_ref,
                 vals_ref, na_ref, a0_ref, a1_ref, p0_ref, p1_ref, *, final_norm):
    k = pl.program_id(1)
    tb = x_ref.shape[0]
    ec = a0_ref.shape[0]

    @pl.when(k == 0)
    def _route():
        h = _rms(x_ref[...], g_ref[...])
        ht = h.T.astype(MXU_DTYPE)
        ht_ref[...] = ht
        q_ref[...] = jnp.dot(wq_ref[...], ht, preferred_element_type=jnp.float32)
        yt_ref[...] = jnp.zeros_like(yt_ref)
        a1_ref[...] = jnp.zeros_like(a1_ref)
        p0_ref[...] = jnp.zeros_like(p0_ref)
        p1_ref[...] = jnp.zeros_like(p1_ref)
        iota_f = lax.broadcasted_iota(jnp.int32, (PEER_NKEYS, tb), 0).astype(jnp.float32)

        def head_scores(hh, carry):
            scores = []
            all_ok = None
            for p in range(2):
                idx = hh * 2 + p
                start = pl.multiple_of(idx * PEER_NKEYS, PEER_NKEYS)
                qhp = q_ref[pl.ds(start, PEER_NKEYS), :]
                s = jnp.dot(sk_ref[idx], qhp.astype(MXU_DTYPE), preferred_element_type=jnp.float32)
                s = s + 0.0
                scores.append(s)
                rank, ok = _top16_no_ties(s, vals_ref, p, hh)
                all_ok = ok if all_ok is None else jnp.logical_and(all_ok, ok)
                e = jnp.exp(s - vals_ref[p, 0, pl.ds(hh, 1), :])
                if p == 0:
                    rank1_ref[hh] = rank
                    e1_ref[hh] = e
                else:
                    rank2_ref[hh] = rank.astype(ROUTE_DTYPE)
                    e2_ref[hh] = e.astype(ROUTE_DTYPE)

            @pl.when(jnp.logical_not(all_ok))
            def _exact():
                rank1_ref[hh] = _top16(scores[0], iota_f, vals_ref, 0, hh)
                rank2_ref[hh] = _top16(scores[1], iota_f, vals_ref, 1, hh).astype(ROUTE_DTYPE)
            return carry

        lax.fori_loop(0, PEER_HEADS, head_scores, 0)

        pairs = _candidate_pairs()
        v1 = [vals_ref[0, a] for a in range(PEER_TOPK)]
        v2 = [vals_ref[1, b] for b in range(PEER_TOPK)]
        cand = {ab: v1[ab[0]] + v2[ab[1]] for ab in pairs}
        pos = {ab: jnp.full((PEER_HEADS, tb), float((ab[0] + 1) * (ab[1] + 1) - 1), jnp.float32) for ab in pairs}
        for i, x_ab in enumerate(pairs):
            for y_ab in pairs[i + 1:]:
                (a, b), (a2, b2) = x_ab, y_ab
                if (a2 >= a and b2 >= b) or (a2 <= a and b2 <= b):
                    continue
                x_first = cand[x_ab] >= cand[y_ab]
                pos[y_ab] = pos[y_ab] + jnp.where(x_first, 1.0, 0.0)
                pos[x_ab] = pos[x_ab] + jnp.where(x_first, 0.0, 1.0)
        c00 = cand[(0, 0)]
        z = jnp.zeros((PEER_HEADS, tb), jnp.float32)
        n_a = [jnp.zeros((PEER_HEADS, tb), jnp.float32) for _ in range(PEER_TOPK)]
        for (a, b) in pairs:
            chosen = pos[(a, b)] < float(PEER_TOPK)
            n_a[a] = n_a[a] + jnp.where(chosen, 1.0, 0.0)
            z = z + jnp.where(chosen, jnp.exp(cand[(a, b)] - c00), 0.0)
        inv_z = 1.0 / z
        for a in range(PEER_TOPK):
            na_ref[a] = n_a[a]
        na_ref[PEER_TOPK] = inv_z

        def head_rows(hh, carry):
            rank1 = rank1_ref[hh]
            n_i = jnp.zeros((PEER_NKEYS, tb), jnp.float32)
            for a in range(PEER_TOPK):
                n_i = jnp.where(rank1 == float(a), na_ref[a, pl.ds(hh, 1), :], n_i)
            n_ref[hh] = n_i
            e1_ref[hh] = e1_ref[hh] * na_ref[PEER_TOPK, pl.ds(hh, 1), :]
            return carry

        lax.fori_loop(0, PEER_HEADS, head_rows, 0)

    routing = (rank2_ref, e2_ref, n_ref, e1_ref)
    ht = ht_ref[...]
    y_even = jnp.dot(vt_ref[:, 0:ec], p0_ref[...], preferred_element_type=jnp.float32)
    a0_ref[...] = jnp.dot(u_ref[0:ec, :], ht, preferred_element_type=jnp.float32)
    _routed_chunk(a1_ref, p1_ref, 2 * k - 1, *routing)
    y_odd = jnp.dot(vt_ref[:, ec:2 * ec], p1_ref[...], preferred_element_type=jnp.float32)
    a1_ref[...] = jnp.dot(u_ref[ec:2 * ec, :], ht, preferred_element_type=jnp.float32)
    _routed_chunk(a0_ref, p0_ref, 2 * k, *routing)
    yt_ref[...] += y_even + y_odd

    @pl.when(k == pl.num_programs(1) - 1)
    def _finish():
        out = x_ref[...] + yt_ref[...].T
        if final_norm:
            out = _rms(out, gf_ref[...])
        o_ref[...] = out


def _peer(x, g, w_q, subkeys, u_tab, v_tab, g_final, final_norm):
    T, D = x.shape
    tb = min(PEER_TB, T)
    ec = PEER_EC
    assert T % tb == 0 and PEER_N_EXPERTS % (2 * ec) == 0 and ec % PEER_NKEYS == 0
    n_pair = PEER_N_EXPERTS // (2 * ec)
    hq = PEER_HEADS * PEER_DKEY
    wq_t = w_q.T.astype(MXU_DTYPE)
    sk = subkeys.reshape(PEER_HEADS * 2, PEER_NKEYS, PEER_DKEY // 2).astype(MXU_DTYPE)
    u_b = u_tab.astype(MXU_DTYPE)
    vt_b = v_tab.T.astype(MXU_DTYPE)
    head_tile = (PEER_HEADS, PEER_NKEYS, tb)
    return pl.pallas_call(
        functools.partial(_peer_kernel, final_norm=final_norm),
        grid=(T // tb, n_pair + 1),
        in_specs=[
            pl.BlockSpec((tb, D), lambda t, k: (t, 0)),
            pl.BlockSpec((1, D), lambda t, k: (0, 0)),
            pl.BlockSpec((hq, D), lambda t, k: (0, 0)),
            pl.BlockSpec((PEER_HEADS * 2, PEER_NKEYS, PEER_DKEY // 2), lambda t, k: (0, 0, 0)),
            pl.BlockSpec((2 * ec, D), lambda t, k: (jnp.minimum(k, n_pair - 1), 0)),
            pl.BlockSpec((D, 2 * ec), lambda t, k: (0, jnp.maximum(k - 1, 0))),
            pl.BlockSpec((1, D), lambda t, k: (0, 0)),
        ],
        out_specs=pl.BlockSpec((tb, D), lambda t, k: (t, 0)),
        out_shape=jax.ShapeDtypeStruct((T, D), jnp.float32),
        scratch_shapes=[
            pltpu.VMEM((D, tb), MXU_DTYPE),
            pltpu.VMEM((D, tb), jnp.float32),
            pltpu.VMEM((hq, tb), jnp.float32),
            pltpu.VMEM(head_tile, jnp.float32),
            pltpu.VMEM(head_tile, ROUTE_DTYPE),
            pltpu.VMEM(head_tile, jnp.float32),
            pltpu.VMEM(head_tile, ROUTE_DTYPE),
            pltpu.VMEM(head_tile, jnp.float32),
            pltpu.VMEM((2, PEER_TOPK, PEER_HEADS, tb), jnp.float32),
            pltpu.VMEM((PEER_TOPK + 1, PEER_HEADS, tb), jnp.float32),
            pltpu.VMEM((ec, tb), jnp.float32),
            pltpu.VMEM((ec, tb), jnp.float32),
            pltpu.VMEM((ec, tb), MXU_DTYPE),
            pltpu.VMEM((ec, tb), MXU_DTYPE),
        ],
        compiler_params=pltpu.CompilerParams(dimension_semantics=("arbitrary", "arbitrary"),
                                             vmem_limit_bytes=VMEM_LIMIT),
        name="peer_final" if final_norm else "peer",
    )(x, g.reshape(1, D), wq_t, sk, u_b, vt_b, g_final.reshape(1, D))


def _qkv_kernel(x_ref, g_ref, w_ref, q_ref, k_ref, v_ref):
    h = _rms(x_ref[...], g_ref[...])
    qkv = _mm(h, w_ref[...])
    d = q_ref.shape[1]
    q_ref[...] = qkv[:, :d]
    k_ref[...] = qkv[:, d:2 * d]
    v_ref[...] = qkv[:, 2 * d:]


def _qkv(x, g, w_qkv):
    T, D = x.shape
    tm = min(MM_TM, T)
    aw = MOBA_HEADS * MOBA_HEAD_DIM
    out = jax.ShapeDtypeStruct((T, aw), jnp.float32)
    return pl.pallas_call(
        _qkv_kernel,
        grid=(T // tm,),
        in_specs=[pl.BlockSpec((tm, D), lambda i: (i, 0)),
                  pl.BlockSpec((1, D), lambda i: (0, 0)),
                  pl.BlockSpec((D, 3 * aw), lambda i: (0, 0))],
        out_specs=[pl.BlockSpec((tm, aw), lambda i: (i, 0))] * 3,
        out_shape=[out, out, out],
        compiler_params=pltpu.CompilerParams(dimension_semantics=("arbitrary",),
                                             vmem_limit_bytes=VMEM_LIMIT),
        name="qkv",
    )(x, g.reshape(1, D), w_qkv.astype(MXU_DTYPE))


def _oproj_kernel(x_ref, o_ref, w_ref, y_ref):
    y_ref[...] = x_ref[...] + _mm(o_ref[...], w_ref[...])


def _oproj(x, o, w_o):
    T, D = x.shape
    tm = min(MM_TM, T)
    aw = o.shape[1]
    return pl.pallas_call(
        _oproj_kernel,
        grid=(T // tm,),
        in_specs=[pl.BlockSpec((tm, D), lambda i: (i, 0)),
                  pl.BlockSpec((tm, aw), lambda i: (i, 0)),
                  pl.BlockSpec((aw, D), lambda i: (0, 0))],
        out_specs=pl.BlockSpec((tm, D), lambda i: (i, 0)),
        out_shape=jax.ShapeDtypeStruct((T, D), jnp.float32),
        compiler_params=pltpu.CompilerParams(dimension_semantics=("arbitrary",),
                                             vmem_limit_bytes=VMEM_LIMIT),
        name="oproj",
    )(x, o, w_o.astype(MXU_DTYPE))


def _t5_bucket(dist):
    n = jnp.maximum(dist, 0)
    max_exact = REL_BUCKETS // 2
    nf = jnp.maximum(n, 1).astype(jnp.float32)
    log_b = max_exact + (jnp.log(nf / max_exact) / math.log(REL_MAX_DIST / max_exact)
                         * (REL_BUCKETS - max_exact)).astype(jnp.int32)
    return jnp.where(n < max_exact, n, jnp.minimum(log_b, REL_BUCKETS - 1))


def _bias_kernel(rb_ref, o_ref):
    h = pl.program_id(0)
    dblk = pl.program_id(1)
    key = lax.broadcasted_iota(jnp.int32, (MOBA_BLOCK, MOBA_BLOCK), 0)
    qry = lax.broadcasted_iota(jnp.int32, (MOBA_BLOCK, MOBA_BLOCK), 1)
    dist = dblk * MOBA_BLOCK + qry - key
    bucket = _t5_bucket(dist)
    val = jnp.zeros((MOBA_BLOCK, MOBA_BLOCK), jnp.float32)
    for b in range(REL_BUCKETS):
        val = jnp.where(bucket == b, rb_ref[b, h], val)
    o_ref[0, 0] = jnp.where(dist >= 0, val * LOG2E, NEG)


def _bias_tiles(rel_bias):
    last_tile_min_dist = (BIAS_TILES - 1) * MOBA_BLOCK - (MOBA_BLOCK - 1)
    max_exact = REL_BUCKETS // 2
    first_last_bucket = max_exact * (REL_MAX_DIST / max_exact) ** ((REL_BUCKETS - 1 - max_exact) / (REL_BUCKETS - max_exact))
    assert last_tile_min_dist > first_last_bucket + 1
    return pl.pallas_call(
        _bias_kernel,
        grid=(MOBA_HEADS, BIAS_TILES),
        in_specs=[pl.BlockSpec(memory_space=pltpu.SMEM)],
        out_specs=pl.BlockSpec((1, 1, MOBA_BLOCK, MOBA_BLOCK), lambda h, d: (h, d, 0, 0)),
        out_shape=jax.ShapeDtypeStruct((MOBA_HEADS, BIAS_TILES, MOBA_BLOCK, MOBA_BLOCK), jnp.float32),
        compiler_params=pltpu.CompilerParams(dimension_semantics=("arbitrary", "arbitrary")),
        name="t5_bias_tiles",
    )(rel_bias)


def _attn_kernel(qa_ref, qb_ref, k_ref, v_ref, bias_ref, o_ref,
                 kb_ref, vt_ref, kmean_ref, qs_ref, sel_ref, m_ref, l_ref, acc_ref, s_ref):
    j = pl.program_id(2)
    S = k_ref.shape[1]
    n_blk = S // MOBA_BLOCK
    tq = qa_ref.shape[1]

    @pl.when(j == 0)
    def _prep():
        for n in range(n_blk):
            kblk = k_ref[0, n * MOBA_BLOCK:(n + 1) * MOBA_BLOCK, :]
            kb_ref[n * MOBA_BLOCK:(n + 1) * MOBA_BLOCK, :] = kblk.astype(MXU_DTYPE)
            kmean_ref[n:n + 1, :] = jnp.mean(kblk, axis=0, keepdims=True)
            vt_ref[:, n * MOBA_BLOCK:(n + 1) * MOBA_BLOCK] = (
                v_ref[0, n * MOBA_BLOCK:(n + 1) * MOBA_BLOCK, :].T.astype(MXU_DTYPE))

    owns = (j, n_blk - 1 - j)
    low_head = lax.broadcasted_iota(jnp.int32, (tq, LANES), 1) < MOBA_HEAD_DIM
    blk = lax.broadcasted_iota(jnp.int32, (n_blk, tq), 0)
    for t, q_ref in enumerate((qa_ref, qb_ref)):
        q = q_ref[0]
        past = blk < owns[t]
        for e in range(2):
            q_e = jnp.where(low_head, q, 0.0) if e == 0 else jnp.where(low_head, 0.0, q)
            gate = jnp.where(past, _mm_nt(kmean_ref[...], q_e), -jnp.inf)
            beaten = jnp.zeros((n_blk, tq), jnp.float32)
            for m in range(n_blk):
                gm = gate[m:m + 1, :]
                beats = (gm > gate) | ((gm == gate) & (blk > m))
                beaten = beaten + jnp.where(beats, 1.0, 0.0)
            chosen = (past & (beaten < float(MOBA_TOPK))) | (blk == owns[t])
            sel_ref[t, e] = jnp.where(chosen, 1.0, 0.0)
            qs_ref[t, e] = (q_e * (MOBA_HEAD_DIM ** -0.5 * LOG2E)).astype(MXU_DTYPE)
            m_ref[t, e] = jnp.full((1, tq), NEG, jnp.float32)
            l_ref[t, e] = jnp.zeros((1, tq), jnp.float32)
            acc_ref[t, e] = jnp.zeros((LANES, tq), jnp.float32)

    def unit(u):
        is_a = u <= j
        t = jnp.where(is_a, 0, 1)
        n = jnp.where(is_a, j - u, n_blk - u)
        own = jnp.where(is_a, j, n_blk - 1 - j)
        return t, n, jnp.minimum(own - n, BIAS_TILES - 1)

    for u in range(n_blk + 1):
        t, n, tile = unit(u)
        k_n = kb_ref[pl.ds(pl.multiple_of(n * MOBA_BLOCK, MOBA_BLOCK), MOBA_BLOCK), :]
        for e in range(2):
            st = _mm_nt(k_n, qs_ref[t, e]) + bias_ref[e, tile]
            st = jnp.where(sel_ref[t, e, pl.ds(n, 1), :] > 0.0, st, NEG)
            s_ref[u, e] = st
            m_ref[t, e] = jnp.maximum(m_ref[t, e], jnp.max(st, axis=0, keepdims=True))

    for u in range(n_blk + 1):
        t, n, _ = unit(u)
        vt_n = vt_ref[:, pl.ds(pl.multiple_of(n * MOBA_BLOCK, MOBA_BLOCK), MOBA_BLOCK)]
        for e in range(2):
            p = jnp.exp2(s_ref[u, e] - m_ref[t, e])
            l_ref[t, e] = l_ref[t, e] + jnp.sum(p, axis=0, keepdims=True)
            acc_ref[t, e] = acc_ref[t, e] + jnp.dot(vt_n, p.astype(MXU_DTYPE), preferred_element_type=jnp.float32)

    for t in range(2):
        o0 = acc_ref[t, 0] / l_ref[t, 0]
        o1 = acc_ref[t, 1] / l_ref[t, 1]
        ot = jnp.concatenate([o0[:MOBA_HEAD_DIM], o1[MOBA_HEAD_DIM:]], axis=0)
        row0 = pl.multiple_of(owns[t] * tq, tq)
        o_ref[0, pl.ds(row0, tq), :] = ot.T


def _attention(q, k, v, bias_tiles):
    B, S, aw = q.shape
    tq = ATT_TQ
    assert S % (2 * MOBA_BLOCK) == 0 and tq == MOBA_BLOCK
    n_blk = S // MOBA_BLOCK
    n_pair = aw // LANES
    return pl.pallas_call(
        _attn_kernel,
        grid=(n_pair, B, n_blk // 2),
        in_specs=[
            pl.BlockSpec((1, tq, LANES), lambda hp, b, j: (b, j, hp)),
            pl.BlockSpec((1, tq, LANES), lambda hp, b, j: (b, n_blk - 1 - j, hp)),
            pl.BlockSpec((1, S, LANES), lambda hp, b, j: (b, 0, hp)),
            pl.BlockSpec((1, S, LANES), lambda hp, b, j: (b, 0, hp)),
            pl.BlockSpec((2, BIAS_TILES, MOBA_BLOCK, MOBA_BLOCK), lambda hp, b, j: (hp, 0, 0, 0)),
        ],
        out_specs=pl.BlockSpec((1, S, LANES), lambda hp, b, j: (b, 0, hp)),
        out_shape=jax.ShapeDtypeStruct((B, S, aw), jnp.float32),
        scratch_shapes=[
            pltpu.VMEM((S, LANES), MXU_DTYPE),
            pltpu.VMEM((LANES, S), MXU_DTYPE),
            pltpu.VMEM((n_blk, LANES), jnp.float32),
            pltpu.VMEM((2, 2, tq, LANES), MXU_DTYPE),
            pltpu.VMEM((2, 2, n_blk, tq), jnp.float32),
            pltpu.VMEM((2, 2, 1, tq), jnp.float32),
            pltpu.VMEM((2, 2, 1, tq), jnp.float32),
            pltpu.VMEM((2, 2, LANES, tq), jnp.float32),
            pltpu.VMEM((n_blk + 1, 2, MOBA_BLOCK, tq), jnp.float32),
        ],
        compiler_params=pltpu.CompilerParams(dimension_semantics=("arbitrary", "arbitrary", "arbitrary"),
                                             vmem_limit_bytes=VMEM_LIMIT),
        name="moba_attention",
    )(q, q, k, v, bias_tiles)


def kernel(x, ev_w_in, ev_conv_w, ev_conv_b, ev_conv_ln_g, ev_conv_ln_b, ev_sgu_ln_g, ev_sgu_ln_b, ev_sgu_w, ev_sgu_b, ev_w_out, od_w_qkv, od_w_o, rel_bias, peer_w_q, peer_subkeys, peer_u, peer_v, norm_mix_g, norm_ffn_g, norm_final_g):
    B, S, D = x.shape
    T = B * S
    x = _mixer(x, norm_mix_g[0], ev_w_in[0], ev_conv_w[0], ev_conv_b[0], ev_conv_ln_g[0], ev_conv_ln_b[0],
               ev_sgu_ln_g[0], ev_sgu_ln_b[0], ev_sgu_w[0], ev_sgu_b[0], ev_w_out[0])
    xt = _peer(x.reshape(T, D), norm_ffn_g[0], peer_w_q[0], peer_subkeys[0], peer_u[0], peer_v[0],
               norm_final_g, final_norm=False)
    q, k, v = _qkv(xt, norm_mix_g[1], od_w_qkv[0])
    aw = MOBA_HEADS * MOBA_HEAD_DIM
    o = _attention(q.reshape(B, S, aw), k.reshape(B, S, aw), v.reshape(B, S, aw), _bias_tiles(rel_bias))
    xt = _oproj(xt, o.reshape(T, aw), od_w_o[0])
    xt = _peer(xt, norm_ffn_g[1], peer_w_q[1], peer_subkeys[1], peer_u[1], peer_v[1],
               norm_final_g, final_norm=True)
    return xt.reshape(B, S, D)
```

```python
import functools
import math

import jax
import jax.numpy as jnp
from jax import lax
from jax.experimental import pallas as pl
from jax.experimental.pallas import tpu as pltpu

D_MODEL = 1024
A_CH = 512
CONV_W = 31
B_HEADS = 8
B_HEAD_DIM = 64
B_CH = B_HEADS * B_HEAD_DIM
SGU_CHUNK = 128
MIX_IN = 2 * A_CH + 2 * B_CH
MOBA_HEADS = 16
MOBA_HEAD_DIM = 64
MOBA_BLOCK = 256
MOBA_TOPK = 3
REL_BUCKETS = 32
REL_MAX_DIST = 2048
PEER_HEADS = 8
PEER_NKEYS = 128
PEER_N_EXPERTS = PEER_NKEYS * PEER_NKEYS
PEER_DKEY = 256
PEER_TOPK = 16
EPS = 1e-6
NEG = -1e30
LOG2E = math.log2(math.e)

LANES = 128
MXU_DTYPE = jnp.bfloat16
ROUTE_DTYPE = jnp.bfloat16
VMEM_LIMIT = 56 * 1024 * 1024

MIX_TS = 512
CONV_HALO = 32
PEER_TB = 512
PEER_EC = 512
MM_TM = 512
ATT_TQ = MOBA_BLOCK
BIAS_TILES = 8


def _rms(x, g):
    return x * lax.rsqrt(jnp.mean(x * x, axis=-1, keepdims=True) + EPS) * g


def _layer_norm(x, g, b):
    mu = jnp.mean(x, axis=-1, keepdims=True)
    var = jnp.mean(jnp.square(x - mu), axis=-1, keepdims=True)
    return (x - mu) * lax.rsqrt(var + EPS) * g + b


def _gelu(x):
    cdf = 0.5 * (1.0 + jnp.tanh(math.sqrt(2.0 / math.pi) * (x + 0.044715 * (x * x * x))))
    return x * cdf


def _mm(a, b):
    return jnp.dot(a.astype(MXU_DTYPE), b.astype(MXU_DTYPE), preferred_element_type=jnp.float32)


def _mm_nt(a, b):
    return lax.dot_general(a.astype(MXU_DTYPE), b.astype(MXU_DTYPE), (((1,), (1,)), ((), ())),
                           preferred_element_type=jnp.float32)


def _mixer_kernel(x_ref, g_ref, win_ref, cw_ref, cb_ref, clg_ref, clb_ref, slg_ref, slb_ref,
                  sw_ref, sb_ref, wout_ref, o_ref, abuf_ref, s_ref, shift0_ref, shift1_ref):
    ts = x_ref.shape[1]
    x = x_ref[0]
    h = _rms(x, g_ref[...])
    z = _mm(h, win_ref[...])
    a = z[:, :A_CH] * jax.nn.sigmoid(z[:, A_CH:2 * A_CH])

    @pl.when(pl.program_id(1) == 0)
    def _():
        abuf_ref[0:CONV_HALO, :] = jnp.zeros((CONV_HALO, A_CH), jnp.float32)

    @pl.when(pl.program_id(1) > 0)
    def _():
        abuf_ref[0:CONV_HALO, :] = abuf_ref[ts:ts + CONV_HALO, :]

    abuf_ref[CONV_HALO:CONV_HALO + ts, :] = a
    acc = jnp.broadcast_to(cb_ref[...], (ts, A_CH))
    first = CONV_HALO - (CONV_W - 1)
    for r in range(8):
        taps = [w for w in range(CONV_W) if (first + w) % 8 == r]
        span = max((first + w) // 8 for w in taps) * 8 + ts
        shifted_ref = (shift0_ref, shift1_ref)[r % 2]
        shifted_ref[0:span, :] = abuf_ref[r:r + span, :]
        for w in taps:
            base = ((first + w) // 8) * 8
            acc = acc + cw_ref[w:w + 1, :] * shifted_ref[base:base + ts, :]
    a_n = _layer_norm(acc, clg_ref[...], clb_ref[...])
    a_out = a_n * jax.nn.sigmoid(a_n)

    u = _gelu(z[:, 2 * A_CH:2 * A_CH + B_CH])
    v = _layer_norm(_gelu(z[:, 2 * A_CH + B_CH:]), slg_ref[...], slb_ref[...])
    row = lax.broadcasted_iota(jnp.int32, (SGU_CHUNK, SGU_CHUNK), 0)
    col = lax.broadcasted_iota(jnp.int32, (SGU_CHUNK, SGU_CHUNK), 1)
    low_head = lax.broadcasted_iota(jnp.int32, (SGU_CHUNK, LANES), 1) < B_HEAD_DIM
    wm = [jnp.where(col <= row, sw_ref[hh], 0.0).astype(MXU_DTYPE) for hh in range(B_HEADS)]
    for c in range(ts // SGU_CHUNK):
        for pr in range(B_HEADS // 2):
            vp = v[c * SGU_CHUNK:(c + 1) * SGU_CHUNK, pr * LANES:(pr + 1) * LANES].astype(MXU_DTYPE)
            s0 = jnp.dot(wm[2 * pr], vp, preferred_element_type=jnp.float32)
            s1 = jnp.dot(wm[2 * pr + 1], vp, preferred_element_type=jnp.float32)
            s_ref[c * SGU_CHUNK:(c + 1) * SGU_CHUNK, pr * LANES:(pr + 1) * LANES] = (
                jnp.where(low_head, s0, s1) + sb_ref[:, pr * LANES:(pr + 1) * LANES])
    bo = u * s_ref[...]
    out = _mm(a_out, wout_ref[0:A_CH, :]) + _mm(bo, wout_ref[A_CH:, :])
    o_ref[0] = x + out


def _mixer(x, g, w_in, conv_w, conv_b, cln_g, cln_b, sln_g, sln_b, sgu_w, sgu_b, w_out):
    B, S, D = x.shape
    ts = min(MIX_TS, S)
    assert S % ts == 0 and ts % SGU_CHUNK == 0
    row2 = lambda a: a.reshape(1, -1)
    sb_exp = jnp.repeat(sgu_b.T, B_HEAD_DIM, axis=1)
    const = lambda shape: pl.BlockSpec(shape, lambda b, s: (0,) * len(shape))
    return pl.pallas_call(
        _mixer_kernel,
        grid=(B, S // ts),
        in_specs=[
            pl.BlockSpec((1, ts, D), lambda b, s: (b, s, 0)),
            const((1, D)), const((D, MIX_IN)), const((CONV_W, A_CH)), const((1, A_CH)),
            const((1, A_CH)), const((1, A_CH)), const((1, B_CH)), const((1, B_CH)),
            const((B_HEADS, SGU_CHUNK, SGU_CHUNK)), const((SGU_CHUNK, B_CH)), const((A_CH + B_CH, D)),
        ],
        out_specs=pl.BlockSpec((1, ts, D), lambda b, s: (b, s, 0)),
        out_shape=jax.ShapeDtypeStruct((B, S, D), jnp.float32),
        scratch_shapes=[pltpu.VMEM((ts + CONV_HALO, A_CH), jnp.float32),
                        pltpu.VMEM((ts, B_CH), jnp.float32),
                        pltpu.VMEM((ts + CONV_HALO, A_CH), jnp.float32),
                        pltpu.VMEM((ts + CONV_HALO, A_CH), jnp.float32)],
        compiler_params=pltpu.CompilerParams(dimension_semantics=("arbitrary", "arbitrary"),
                                             vmem_limit_bytes=VMEM_LIMIT),
        name="mixer0",
    )(x, row2(g), w_in.astype(MXU_DTYPE), conv_w, row2(conv_b), row2(cln_g), row2(cln_b),
      row2(sln_g), row2(sln_b), sgu_w, sb_exp, w_out.astype(MXU_DTYPE))


def _candidate_pairs():
    return [(a, b) for a in range(PEER_TOPK) for b in range(PEER_TOPK) if (a + 1) * (b + 1) <= PEER_TOPK]


def _top16(s, iota_f, vals_ref, p, hh):
    work = s
    rank = jnp.full(s.shape, float(PEER_TOPK), jnp.float32)
    for r in range(PEER_TOPK):
        m = jnp.max(work, axis=0, keepdims=True)
        first = jnp.min(jnp.where(work == m, iota_f, float(PEER_NKEYS)), axis=0, keepdims=True)
        sel = iota_f == first
        rank = jnp.where(sel, float(r), rank)
        work = jnp.where(sel, -jnp.inf, work)
        vals_ref[p, r, pl.ds(hh, 1), :] = m
    return rank


def _gelu_sigmoid_form(x):
    k0 = -2.0 * math.sqrt(2.0 / math.pi) * math.log2(math.e)
    u = x * (k0 + (k0 * 0.044715) * (x * x))
    return x / (1.0 + jnp.exp2(u))


TOP16_MARK = 2.0 ** 127


def _top16_no_ties(s, vals_ref, p, hh):
    work = s
    for r in range(PEER_TOPK):
        m = jnp.max(work, axis=0, keepdims=True)
        work = jnp.where(work == m, -(1.0 + (PEER_TOPK - 1 - r) / PEER_TOPK) * TOP16_MARK, work)
        vals_ref[p, r, pl.ds(hh, 1), :] = m
    marked = work <= -TOP16_MARK
    rank = jnp.where(marked, (2.0 * PEER_TOPK - 1.0) + work * (PEER_TOPK / TOP16_MARK), float(PEER_TOPK))
    n_marked = jnp.sum(jnp.where(marked, 1.0, 0.0), axis=0, keepdims=True)
    ok = jnp.max(jnp.abs(n_marked - float(PEER_TOPK))) == 0.0
    return rank, ok


def _routed_chunk(a_ref, p_ref, chunk, rank2_ref, e2_ref, n_ref, e1_ref):
    ec, tb = a_ref.shape
    rows_per_tile = 16
    for g in range(ec // PEER_NKEYS):
        i = chunk * (ec // PEER_NKEYS) + g
        n_rows = [n_ref[hh, pl.ds(i, 1), :] for hh in range(PEER_HEADS)]
        e1_rows = [e1_ref[hh, pl.ds(i, 1), :] for hh in range(PEER_HEADS)]
        for lt in range(tb // LANES):
            cols = slice(lt * LANES, (lt + 1) * LANES)
            n_b = [jnp.broadcast_to(n_rows[hh][:, cols], (rows_per_tile, LANES)).astype(ROUTE_DTYPE)
                   for hh in range(PEER_HEADS)]
            e1_b = [jnp.broadcast_to(e1_rows[hh][:, cols], (rows_per_tile, LANES)).astype(ROUTE_DTYPE)
                    for hh in range(PEER_HEADS)]
            for sg in range(PEER_NKEYS // rows_per_tile):
                rows = slice(sg * rows_per_tile, (sg + 1) * rows_per_tile)
                wgt = None
                for hh in range(PEER_HEADS):
                    term = jnp.where(rank2_ref[hh, rows, cols] < n_b[hh], e2_ref[hh, rows, cols],
                                     jnp.zeros((), ROUTE_DTYPE)) * e1_b[hh]
                    wgt = term if wgt is None else wgt + term
                arows = slice(g * PEER_NKEYS + sg * rows_per_tile, g * PEER_NKEYS + (sg + 1) * rows_per_tile)
                act = _gelu_sigmoid_form(a_ref[arows, cols])
                p_ref[arows, cols] = (wgt * act.astype(ROUTE_DTYPE)).astype(p_ref.dtype)


def _peer_kernel(x_ref, g_ref, wq_ref, sk_ref, u_ref, vt_ref, gf_ref, o_ref,
                 ht_ref, yt_ref, q_ref, rank1_ref, rank2_ref, e1_ref, e2_ref, n_ref,
                 vals_ref, na_ref, a0_ref, a1_ref, p0_ref, p1_ref, *, final_norm):
    k = pl.program_id(1)
    tb = x_ref.shape[0]
    ec = a0_ref.shape[0]

    @pl.when(k == 0)
    def _route():
        h = _rms(x_ref[...], g_ref[...])
        ht = h.T.astype(MXU_DTYPE)
        ht_ref[...] = ht
        q_ref[...] = jnp.dot(wq_ref[...], ht, preferred_element_type=jnp.float32)
        yt_ref[...] = jnp.zeros_like(yt_ref)
        iota_f = lax.broadcasted_iota(jnp.int32, (PEER_NKEYS, tb), 0).astype(jnp.float32)

        def head_scores(hh, carry):
            scores = []
            all_ok = None
            for p in range(2):
                idx = hh * 2 + p
                start = pl.multiple_of(idx * PEER_NKEYS, PEER_NKEYS)
                qhp = q_ref[pl.ds(start, PEER_NKEYS), :]
                s = jnp.dot(sk_ref[idx], qhp.astype(MXU_DTYPE), preferred_element_type=jnp.float32)
                s = s + 0.0
                scores.append(s)
                rank, ok = _top16_no_ties(s, vals_ref, p, hh)
                all_ok = ok if all_ok is None else jnp.logical_and(all_ok, ok)
                e = jnp.exp(s - vals_ref[p, 0, pl.ds(hh, 1), :])
                if p == 0:
                    rank1_ref[hh] = rank
                    e1_ref[hh] = e
                else:
                    rank2_ref[hh] = rank.astype(ROUTE_DTYPE)
                    e2_ref[hh] = e.astype(ROUTE_DTYPE)

            @pl.when(jnp.logical_not(all_ok))
            def _exact():
                rank1_ref[hh] = _top16(scores[0], iota_f, vals_ref, 0, hh)
                rank2_ref[hh] = _top16(scores[1], iota_f, vals_ref, 1, hh).astype(ROUTE_DTYPE)
            return carry

        lax.fori_loop(0, PEER_HEADS, head_scores, 0)

        pairs = _candidate_pairs()
        v1 = [vals_ref[0, a] for a in range(PEER_TOPK)]
        v2 = [vals_ref[1, b] for b in range(PEER_TOPK)]
        cand = {ab: v1[ab[0]] + v2[ab[1]] for ab in pairs}
        pos = {ab: jnp.full((PEER_HEADS, tb), float((ab[0] + 1) * (ab[1] + 1) - 1), jnp.float32) for ab in pairs}
        for i, x_ab in enumerate(pairs):
            for y_ab in pairs[i + 1:]:
                (a, b), (a2, b2) = x_ab, y_ab
                if (a2 >= a and b2 >= b) or (a2 <= a and b2 <= b):
                    continue
                x_first = cand[x_ab] >= cand[y_ab]
                pos[y_ab] = pos[y_ab] + jnp.where(x_first, 1.0, 0.0)
                pos[x_ab] = pos[x_ab] + jnp.where(x_first, 0.0, 1.0)
        c00 = cand[(0, 0)]
        z = jnp.zeros((PEER_HEADS, tb), jnp.float32)
        n_a = [jnp.zeros((PEER_HEADS, tb), jnp.float32) for _ in range(PEER_TOPK)]
        for (a, b) in pairs:
            chosen = pos[(a, b)] < float(PEER_TOPK)
            n_a[a] = n_a[a] + jnp.where(chosen, 1.0, 0.0)
            z = z + jnp.where(chosen, jnp.exp(cand[(a, b)] - c00), 0.0)
        inv_z = 1.0 / z
        for a in range(PEER_TOPK):
            na_ref[a] = n_a[a]
        na_ref[PEER_TOPK] = inv_z

        def head_rows(hh, carry):
            rank1 = rank1_ref[hh]
            n_i = jnp.zeros((PEER_NKEYS, tb), jnp.float32)
            for a in range(PEER_TOPK):
                n_i = jnp.where(rank1 == float(a), na_ref[a, pl.ds(hh, 1), :], n_i)
            n_ref[hh] = n_i
            e1_ref[hh] = e1_ref[hh] * na_ref[PEER_TOPK, pl.ds(hh, 1), :]
            return carry

        lax.fori_loop(0, PEER_HEADS, head_rows, 0)

    routing = (rank2_ref, e2_ref, n_ref, e1_ref)
    last = pl.num_programs(1) - 1

    def stage1(a_ref, half):
        a_ref[...] = jnp.dot(u_ref[half * ec:(half + 1) * ec, :], ht_ref[...], preferred_element_type=jnp.float32)

    def stage3(p_ref, half):
        return jnp.dot(vt_ref[:, half * ec:(half + 1) * ec], p_ref[...], preferred_element_type=jnp.float32)

    @pl.when(k == 0)
    def _fill():
        stage1(a0_ref, 0)
        stage1(a1_ref, 1)
        _routed_chunk(a0_ref, p0_ref, 0, *routing)

    @pl.when(jnp.logical_and(k > 0, k < last))
    def _steady():
        y_even = stage3(p0_ref, 0)
        stage1(a0_ref, 0)
        _routed_chunk(a1_ref, p1_ref, 2 * k - 1, *routing)
        y_odd = stage3(p1_ref, 1)
        stage1(a1_ref, 1)
        _routed_chunk(a0_ref, p0_ref, 2 * k, *routing)
        yt_ref[...] += y_even + y_odd

    @pl.when(k == last)
    def _drain():
        y_even = stage3(p0_ref, 0)
        _routed_chunk(a1_ref, p1_ref, 2 * k - 1, *routing)
        yt = yt_ref[...] + y_even + stage3(p1_ref, 1)
        out = x_ref[...] + yt.T
        if final_norm:
            out = _rms(out, gf_ref[...])
        o_ref[...] = out


def _peer(x, g, w_q, subkeys, u_tab, v_tab, g_final, final_norm):
    T, D = x.shape
    tb = min(PEER_TB, T)
    ec = PEER_EC
    assert T % tb == 0 and PEER_N_EXPERTS % (2 * ec) == 0 and ec % PEER_NKEYS == 0
    n_pair = PEER_N_EXPERTS // (2 * ec)
    hq = PEER_HEADS * PEER_DKEY
    wq_t = w_q.T.astype(MXU_DTYPE)
    sk = subkeys.reshape(PEER_HEADS * 2, PEER_NKEYS, PEER_DKEY // 2).astype(MXU_DTYPE)
    u_b = u_tab.astype(MXU_DTYPE)
    vt_b = v_tab.T.astype(MXU_DTYPE)
    head_tile = (PEER_HEADS, PEER_NKEYS, tb)
    return pl.pallas_call(
        functools.partial(_peer_kernel, final_norm=final_norm),
        grid=(T // tb, n_pair + 1),
        in_specs=[
            pl.BlockSpec((tb, D), lambda t, k: (t, 0)),
            pl.BlockSpec((1, D), lambda t, k: (0, 0)),
            pl.BlockSpec((hq, D), lambda t, k: (0, 0)),
            pl.BlockSpec((PEER_HEADS * 2, PEER_NKEYS, PEER_DKEY // 2), lambda t, k: (0, 0, 0)),
            pl.BlockSpec((2 * ec, D), lambda t, k: (jnp.minimum(k, n_pair - 1), 0)),
            pl.BlockSpec((D, 2 * ec), lambda t, k: (0, jnp.maximum(k - 1, 0))),
            pl.BlockSpec((1, D), lambda t, k: (0, 0)),
        ],
        out_specs=pl.BlockSpec((tb, D), lambda t, k: (t, 0)),
        out_shape=jax.ShapeDtypeStruct((T, D), jnp.float32),
        scratch_shapes=[
            pltpu.VMEM((D, tb), MXU_DTYPE),
            pltpu.VMEM((D, tb), jnp.float32),
            pltpu.VMEM((hq, tb), jnp.float32),
            pltpu.VMEM(head_tile, jnp.float32),
            pltpu.VMEM(head_tile, ROUTE_DTYPE),
            pltpu.VMEM(head_tile, jnp.float32),
            pltpu.VMEM(head_tile, ROUTE_DTYPE),
            pltpu.VMEM(head_tile, jnp.float32),
            pltpu.VMEM((2, PEER_TOPK, PEER_HEADS, tb), jnp.float32),
            pltpu.VMEM((PEER_TOPK + 1, PEER_HEADS, tb), jnp.float32),
            pltpu.VMEM((ec, tb), jnp.float32),
            pltpu.VMEM((ec, tb), jnp.float32),
            pltpu.VMEM((ec, tb), MXU_DTYPE),
            pltpu.VMEM((ec, tb), MXU_DTYPE),
        ],
        compiler_params=pltpu.CompilerParams(dimension_semantics=("arbitrary", "arbitrary"),
                                             vmem_limit_bytes=VMEM_LIMIT),
        name="peer_final" if final_norm else "peer",
    )(x, g.reshape(1, D), wq_t, sk, u_b, vt_b, g_final.reshape(1, D))


def _qkv_kernel(x_ref, g_ref, w_ref, q_ref, k_ref, v_ref):
    h = _rms(x_ref[...], g_ref[...])
    qkv = _mm(h, w_ref[...])
    d = q_ref.shape[1]
    q_ref[...] = qkv[:, :d]
    k_ref[...] = qkv[:, d:2 * d]
    v_ref[...] = qkv[:, 2 * d:]


def _qkv(x, g, w_qkv):
    T, D = x.shape
    tm = min(MM_TM, T)
    aw = MOBA_HEADS * MOBA_HEAD_DIM
    out = jax.ShapeDtypeStruct((T, aw), jnp.float32)
    return pl.pallas_call(
        _qkv_kernel,
        grid=(T // tm,),
        in_specs=[pl.BlockSpec((tm, D), lambda i: (i, 0)),
                  pl.BlockSpec((1, D), lambda i: (0, 0)),
                  pl.BlockSpec((D, 3 * aw), lambda i: (0, 0))],
        out_specs=[pl.BlockSpec((tm, aw), lambda i: (i, 0))] * 3,
        out_shape=[out, out, out],
        compiler_params=pltpu.CompilerParams(dimension_semantics=("arbitrary",),
                                             vmem_limit_bytes=VMEM_LIMIT),
        name="qkv",
    )(x, g.reshape(1, D), w_qkv.astype(MXU_DTYPE))


def _oproj_kernel(x_ref, o_ref, w_ref, y_ref):
    y_ref[...] = x_ref[...] + _mm(o_ref[...], w_ref[...])


def _oproj(x, o, w_o):
    T, D = x.shape
    tm = min(MM_TM, T)
    aw = o.shape[1]
    return pl.pallas_call(
        _oproj_kernel,
        grid=(T // tm,),
        in_specs=[pl.BlockSpec((tm, D), lambda i: (i, 0)),
                  pl.BlockSpec((tm, aw), lambda i: (i, 0)),
                  pl.BlockSpec((aw, D), lambda i: (0, 0))],
        out_specs=pl.BlockSpec((tm, D), lambda i: (i, 0)),
        out_shape=jax.ShapeDtypeStruct((T, D), jnp.float32),
        compiler_params=pltpu.CompilerParams(dimension_semantics=("arbitrary",),
                                             vmem_limit_bytes=VMEM_LIMIT),
        name="oproj",
    )(x, o, w_o.astype(MXU_DTYPE))


def _t5_bucket(dist):
    n = jnp.maximum(dist, 0)
    max_exact = REL_BUCKETS // 2
    nf = jnp.maximum(n, 1).astype(jnp.float32)
    log_b = max_exact + (jnp.log(nf / max_exact) / math.log(REL_MAX_DIST / max_exact)
                         * (REL_BUCKETS - max_exact)).astype(jnp.int32)
    return jnp.where(n < max_exact, n, jnp.minimum(log_b, REL_BUCKETS - 1))


def _bias_kernel(rb_ref, o_ref):
    h = pl.program_id(0)
    dblk = pl.program_id(1)
    key = lax.broadcasted_iota(jnp.int32, (MOBA_BLOCK, MOBA_BLOCK), 0)
    qry = lax.broadcasted_iota(jnp.int32, (MOBA_BLOCK, MOBA_BLOCK), 1)
    dist = dblk * MOBA_BLOCK + qry - key
    bucket = _t5_bucket(dist)
    val = jnp.zeros((MOBA_BLOCK, MOBA_BLOCK), jnp.float32)
    for b in range(REL_BUCKETS):
        val = jnp.where(bucket == b, rb_ref[b, h], val)
    o_ref[0, 0] = jnp.where(dist >= 0, val * LOG2E, NEG)


def _bias_tiles(rel_bias):
    last_tile_min_dist = (BIAS_TILES - 1) * MOBA_BLOCK - (MOBA_BLOCK - 1)
    max_exact = REL_BUCKETS // 2
    first_last_bucket = max_exact * (REL_MAX_DIST / max_exact) ** ((REL_BUCKETS - 1 - max_exact) / (REL_BUCKETS - max_exact))
    assert last_tile_min_dist > first_last_bucket + 1
    return pl.pallas_call(
        _bias_kernel,
        grid=(MOBA_HEADS, BIAS_TILES),
        in_specs=[pl.BlockSpec(memory_space=pltpu.SMEM)],
        out_specs=pl.BlockSpec((1, 1, MOBA_BLOCK, MOBA_BLOCK), lambda h, d: (h, d, 0, 0)),
        out_shape=jax.ShapeDtypeStruct((MOBA_HEADS, BIAS_TILES, MOBA_BLOCK, MOBA_BLOCK), jnp.float32),
        compiler_params=pltpu.CompilerParams(dimension_semantics=("arbitrary", "arbitrary")),
        name="t5_bias_tiles",
    )(rel_bias)


def _attn_kernel(qa_ref, qb_ref, k_ref, v_ref, bias_ref, o_ref,
                 kb_ref, vt_ref, kmean_ref, qs_ref, sel_ref, m_ref, l_ref, acc_ref, s_ref):
    j = pl.program_id(2)
    S = k_ref.shape[1]
    n_blk = S // MOBA_BLOCK
    tq = qa_ref.shape[1]

    @pl.when(j == 0)
    def _prep():
        for n in range(n_blk):
            kblk = k_ref[0, n * MOBA_BLOCK:(n + 1) * MOBA_BLOCK, :]
            kb_ref[n * MOBA_BLOCK:(n + 1) * MOBA_BLOCK, :] = kblk.astype(MXU_DTYPE)
            kmean_ref[n:n + 1, :] = jnp.mean(kblk, axis=0, keepdims=True)
            vt_ref[:, n * MOBA_BLOCK:(n + 1) * MOBA_BLOCK] = (
                v_ref[0, n * MOBA_BLOCK:(n + 1) * MOBA_BLOCK, :].T.astype(MXU_DTYPE))

    owns = (j, n_blk - 1 - j)
    low_head = lax.broadcasted_iota(jnp.int32, (tq, LANES), 1) < MOBA_HEAD_DIM
    blk = lax.broadcasted_iota(jnp.int32, (n_blk, tq), 0)
    for t, q_ref in enumerate((qa_ref, qb_ref)):
        q = q_ref[0]
        past = blk < owns[t]
        for e in range(2):
            q_e = jnp.where(low_head, q, 0.0) if e == 0 else jnp.where(low_head, 0.0, q)
            gate = jnp.where(past, _mm_nt(kmean_ref[...], q_e), -jnp.inf)
            beaten = jnp.zeros((n_blk, tq), jnp.float32)
            for m in range(n_blk):
                gm = gate[m:m + 1, :]
                beats = (gm > gate) | ((gm == gate) & (blk > m))
                beaten = beaten + jnp.where(beats, 1.0, 0.0)
            chosen = (past & (beaten < float(MOBA_TOPK))) | (blk == owns[t])
            sel_ref[t, e] = jnp.where(chosen, 1.0, 0.0)
            qs_ref[t, e] = (q_e * (MOBA_HEAD_DIM ** -0.5 * LOG2E)).astype(MXU_DTYPE)
            m_ref[t, e] = jnp.full((1, tq), NEG, jnp.float32)
            l_ref[t, e] = jnp.zeros((1, tq), jnp.float32)
            acc_ref[t, e] = jnp.zeros((LANES, tq), jnp.float32)

    def unit(u):
        is_a = u <= j
        t = jnp.where(is_a, 0, 1)
        n = jnp.where(is_a, j - u, n_blk - u)
        own = jnp.where(is_a, j, n_blk - 1 - j)
        return t, n, jnp.minimum(own - n, BIAS_TILES - 1)

    for u in range(n_blk + 1):
        t, n, tile = unit(u)
        k_n = kb_ref[pl.ds(pl.multiple_of(n * MOBA_BLOCK, MOBA_BLOCK), MOBA_BLOCK), :]
        for e in range(2):
            st = _mm_nt(k_n, qs_ref[t, e]) + bias_ref[e, tile]
            st = jnp.where(sel_ref[t, e, pl.ds(n, 1), :] > 0.0, st, NEG)
            s_ref[u, e] = st
            m_ref[t, e] = jnp.maximum(m_ref[t, e], jnp.max(st, axis=0, keepdims=True))

    for u in range(n_blk + 1):
        t, n, _ = unit(u)
        vt_n = vt_ref[:, pl.ds(pl.multiple_of(n * MOBA_BLOCK, MOBA_BLOCK), MOBA_BLOCK)]
        for e in range(2):
            p = jnp.exp2(s_ref[u, e] - m_ref[t, e])
            l_ref[t, e] = l_ref[t, e] + jnp.sum(p, axis=0, keepdims=True)
            acc_ref[t, e] = acc_ref[t, e] + jnp.dot(vt_n, p.astype(MXU_DTYPE), preferred_element_type=jnp.float32)

    for t in range(2):
        o0 = acc_ref[t, 0] / l_ref[t, 0]
        o1 = acc_ref[t, 1] / l_ref[t, 1]
        ot = jnp.concatenate([o0[:MOBA_HEAD_DIM], o1[MOBA_HEAD_DIM:]], axis=0)
        row0 = pl.multiple_of(owns[t] * tq, tq)
        o_ref[0, pl.ds(row0, tq), :] = ot.T


def _attention(q, k, v, bias_tiles):
    B, S, aw = q.shape
    tq = ATT_TQ
    assert S % (2 * MOBA_BLOCK) == 0 and tq == MOBA_BLOCK
    n_blk = S // MOBA_BLOCK
    n_pair = aw // LANES
    return pl.pallas_call(
        _attn_kernel,
        grid=(n_pair, B, n_blk // 2),
        in_specs=[
            pl.BlockSpec((1, tq, LANES), lambda hp, b, j: (b, j, hp)),
            pl.BlockSpec((1, tq, LANES), lambda hp, b, j: (b, n_blk - 1 - j, hp)),
            pl.BlockSpec((1, S, LANES), lambda hp, b, j: (b, 0, hp)),
            pl.BlockSpec((1, S, LANES), lambda hp, b, j: (b, 0, hp)),
            pl.BlockSpec((2, BIAS_TILES, MOBA_BLOCK, MOBA_BLOCK), lambda hp, b, j: (hp, 0, 0, 0)),
        ],
        out_specs=pl.BlockSpec((1, S, LANES), lambda hp, b, j: (b, 0, hp)),
        out_shape=jax.ShapeDtypeStruct((B, S, aw), jnp.float32),
        scratch_shapes=[
            pltpu.VMEM((S, LANES), MXU_DTYPE),
            pltpu.VMEM((LANES, S), MXU_DTYPE),
            pltpu.VMEM((n_blk, LANES), jnp.float32),
            pltpu.VMEM((2, 2, tq, LANES), MXU_DTYPE),
            pltpu.VMEM((2, 2, n_blk, tq), jnp.float32),
            pltpu.VMEM((2, 2, 1, tq), jnp.float32),
            pltpu.VMEM((2, 2, 1, tq), jnp.float32),
            pltpu.VMEM((2, 2, LANES, tq), jnp.float32),
            pltpu.VMEM((n_blk + 1, 2, MOBA_BLOCK, tq), jnp.float32),
        ],
        compiler_params=pltpu.CompilerParams(dimension_semantics=("arbitrary", "arbitrary", "arbitrary"),
                                             vmem_limit_bytes=VMEM_LIMIT),
        name="moba_attention",
    )(q, q, k, v, bias_tiles)


def kernel(x, ev_w_in, ev_conv_w, ev_conv_b, ev_conv_ln_g, ev_conv_ln_b, ev_sgu_ln_g, ev_sgu_ln_b, ev_sgu_w, ev_sgu_b, ev_w_out, od_w_qkv, od_w_o, rel_bias, peer_w_q, peer_subkeys, peer_u, peer_v, norm_mix_g, norm_ffn_g, norm_final_g):
    B, S, D = x.shape
    T = B * S
    x = _mixer(x, norm_mix_g[0], ev_w_in[0], ev_conv_w[0], ev_conv_b[0], ev_conv_ln_g[0], ev_conv_ln_b[0],
               ev_sgu_ln_g[0], ev_sgu_ln_b[0], ev_sgu_w[0], ev_sgu_b[0], ev_w_out[0])
    xt = _peer(x.reshape(T, D), norm_ffn_g[0], peer_w_q[0], peer_subkeys[0], peer_u[0], peer_v[0],
               norm_final_g, final_norm=False)
    q, k, v = _qkv(xt, norm_mix_g[1], od_w_qkv[0])
    aw = MOBA_HEADS * MOBA_HEAD_DIM
    o = _attention(q.reshape(B, S, aw), k.reshape(B, S, aw), v.reshape(B, S, aw), _bias_tiles(rel_bias))
    xt = _oproj(xt, o.reshape(T, aw), od_w_o[0])
    xt = _peer(xt, norm_ffn_g[1], peer_w_q[1], peer_subkeys[1], peer_u[1], peer_v[1],
               norm_final_g, final_norm=True)
    return xt.reshape(B, S, D)
```

```python
import functools
import math

import jax
import jax.numpy as jnp
from jax import lax
from jax.experimental import pallas as pl
from jax.experimental.pallas import tpu as pltpu

D_MODEL = 1024
A_CH = 512
CONV_W = 31
B_HEADS = 8
B_HEAD_DIM = 64
B_CH = B_HEADS * B_HEAD_DIM
SGU_CHUNK = 128
MIX_IN = 2 * A_CH + 2 * B_CH
MOBA_HEADS = 16
MOBA_HEAD_DIM = 64
MOBA_BLOCK = 256
MOBA_TOPK = 3
REL_BUCKETS = 32
REL_MAX_DIST = 2048
PEER_HEADS = 8
PEER_NKEYS = 128
PEER_N_EXPERTS = PEER_NKEYS * PEER_NKEYS
PEER_DKEY = 256
PEER_TOPK = 16
EPS = 1e-6
NEG = -1e30
LOG2E = math.log2(math.e)

LANES = 128
MXU_DTYPE = jnp.bfloat16
ROUTE_DTYPE = jnp.bfloat16
VMEM_LIMIT = 56 * 1024 * 1024

MIX_TS = 512
CONV_HALO = 32
PEER_TB = 512
PEER_EC = 1024
MM_TM = 512
ATT_TQ = MOBA_BLOCK
BIAS_TILES = 8


def _rms(x, g):
    return x * lax.rsqrt(jnp.mean(x * x, axis=-1, keepdims=True) + EPS) * g


def _layer_norm(x, g, b):
    mu = jnp.mean(x, axis=-1, keepdims=True)
    var = jnp.mean(jnp.square(x - mu), axis=-1, keepdims=True)
    return (x - mu) * lax.rsqrt(var + EPS) * g + b


def _gelu(x):
    cdf = 0.5 * (1.0 + jnp.tanh(math.sqrt(2.0 / math.pi) * (x + 0.044715 * (x * x * x))))
    return x * cdf


def _mm(a, b):
    return jnp.dot(a.astype(MXU_DTYPE), b.astype(MXU_DTYPE), preferred_element_type=jnp.float32)


def _mm_nt(a, b):
    return lax.dot_general(a.astype(MXU_DTYPE), b.astype(MXU_DTYPE), (((1,), (1,)), ((), ())),
                           preferred_element_type=jnp.float32)


def _mixer_kernel(x_ref, g_ref, win_ref, cw_ref, cb_ref, clg_ref, clb_ref, slg_ref, slb_ref,
                  sw_ref, sb_ref, wout_ref, o_ref, abuf_ref, s_ref, shift0_ref, shift1_ref):
    ts = x_ref.shape[1]
    x = x_ref[0]
    h = _rms(x, g_ref[...])
    z = _mm(h, win_ref[...])
    a = z[:, :A_CH] * jax.nn.sigmoid(z[:, A_CH:2 * A_CH])

    @pl.when(pl.program_id(1) == 0)
    def _():
        abuf_ref[0:CONV_HALO, :] = jnp.zeros((CONV_HALO, A_CH), jnp.float32)

    @pl.when(pl.program_id(1) > 0)
    def _():
        abuf_ref[0:CONV_HALO, :] = abuf_ref[ts:ts + CONV_HALO, :]

    abuf_ref[CONV_HALO:CONV_HALO + ts, :] = a
    acc = jnp.broadcast_to(cb_ref[...], (ts, A_CH))
    first = CONV_HALO - (CONV_W - 1)
    for r in range(8):
        taps = [w for w in range(CONV_W) if (first + w) % 8 == r]
        span = max((first + w) // 8 for w in taps) * 8 + ts
        shifted_ref = (shift0_ref, shift1_ref)[r % 2]
        shifted_ref[0:span, :] = abuf_ref[r:r + span, :]
        for w in taps:
            base = ((first + w) // 8) * 8
            acc = acc + cw_ref[w:w + 1, :] * shifted_ref[base:base + ts, :]
    a_n = _layer_norm(acc, clg_ref[...], clb_ref[...])
    a_out = a_n * jax.nn.sigmoid(a_n)

    u = _gelu(z[:, 2 * A_CH:2 * A_CH + B_CH])
    v = _layer_norm(_gelu(z[:, 2 * A_CH + B_CH:]), slg_ref[...], slb_ref[...])
    row = lax.broadcasted_iota(jnp.int32, (SGU_CHUNK, SGU_CHUNK), 0)
    col = lax.broadcasted_iota(jnp.int32, (SGU_CHUNK, SGU_CHUNK), 1)
    low_head = lax.broadcasted_iota(jnp.int32, (SGU_CHUNK, LANES), 1) < B_HEAD_DIM
    wm = [jnp.where(col <= row, sw_ref[hh], 0.0).astype(MXU_DTYPE) for hh in range(B_HEADS)]
    for c in range(ts // SGU_CHUNK):
        for pr in range(B_HEADS // 2):
            vp = v[c * SGU_CHUNK:(c + 1) * SGU_CHUNK, pr * LANES:(pr + 1) * LANES].astype(MXU_DTYPE)
            s0 = jnp.dot(wm[2 * pr], vp, preferred_element_type=jnp.float32)
            s1 = jnp.dot(wm[2 * pr + 1], vp, preferred_element_type=jnp.float32)
            s_ref[c * SGU_CHUNK:(c + 1) * SGU_CHUNK, pr * LANES:(pr + 1) * LANES] = (
                jnp.where(low_head, s0, s1) + sb_ref[:, pr * LANES:(pr + 1) * LANES])
    bo = u * s_ref[...]
    out = _mm(a_out, wout_ref[0:A_CH, :]) + _mm(bo, wout_ref[A_CH:, :])
    o_ref[0] = x + out


def _mixer(x, g, w_in, conv_w, conv_b, cln_g, cln_b, sln_g, sln_b, sgu_w, sgu_b, w_out):
    B, S, D = x.shape
    ts = min(MIX_TS, S)
    assert S % ts == 0 and ts % SGU_CHUNK == 0
    row2 = lambda a: a.reshape(1, -1)
    sb_exp = jnp.repeat(sgu_b.T, B_HEAD_DIM, axis=1)
    const = lambda shape: pl.BlockSpec(shape, lambda b, s: (0,) * len(shape))
    return pl.pallas_call(
        _mixer_kernel,
        grid=(B, S // ts),
        in_specs=[
            pl.BlockSpec((1, ts, D), lambda b, s: (b, s, 0)),
            const((1, D)), const((D, MIX_IN)), const((CONV_W, A_CH)), const((1, A_CH)),
            const((1, A_CH)), const((1, A_CH)), const((1, B_CH)), const((1, B_CH)),
            const((B_HEADS, SGU_CHUNK, SGU_CHUNK)), const((SGU_CHUNK, B_CH)), const((A_CH + B_CH, D)),
        ],
        out_specs=pl.BlockSpec((1, ts, D), lambda b, s: (b, s, 0)),
        out_shape=jax.ShapeDtypeStruct((B, S, D), jnp.float32),
        scratch_shapes=[pltpu.VMEM((ts + CONV_HALO, A_CH), jnp.float32),
                        pltpu.VMEM((ts, B_CH), jnp.float32),
                        pltpu.VMEM((ts + CONV_HALO, A_CH), jnp.float32),
                        pltpu.VMEM((ts + CONV_HALO, A_CH), jnp.float32)],
        compiler_params=pltpu.CompilerParams(dimension_semantics=("arbitrary", "arbitrary"),
                                             vmem_limit_bytes=VMEM_LIMIT),
        name="mixer0",
    )(x, row2(g), w_in.astype(MXU_DTYPE), conv_w, row2(conv_b), row2(cln_g), row2(cln_b),
      row2(sln_g), row2(sln_b), sgu_w, sb_exp, w_out.astype(MXU_DTYPE))


def _candidate_pairs():
    return [(a, b) for a in range(PEER_TOPK) for b in range(PEER_TOPK) if (a + 1) * (b + 1) <= PEER_TOPK]


def _top16(s, iota_f, vals_ref, p, hh):
    work = s
    rank = jnp.full(s.shape, float(PEER_TOPK), jnp.float32)
    for r in range(PEER_TOPK):
        m = jnp.max(work, axis=0, keepdims=True)
        first = jnp.min(jnp.where(work == m, iota_f, float(PEER_NKEYS)), axis=0, keepdims=True)
        sel = iota_f == first
        rank = jnp.where(sel, float(r), rank)
        work = jnp.where(sel, -jnp.inf, work)
        vals_ref[p, r, pl.ds(hh, 1), :] = m
    return rank


def _gelu_sigmoid_form(x):
    k0 = -2.0 * math.sqrt(2.0 / math.pi) * math.log2(math.e)
    u = x * (k0 + (k0 * 0.044715) * (x * x))
    return x / (1.0 + jnp.exp2(u))


TOP16_MARK = 2.0 ** 127


def _top16_no_ties(s, vals_ref, p, hh):
    work = s
    for r in range(PEER_TOPK):
        m = jnp.max(work, axis=0, keepdims=True)
        work = jnp.where(work == m, -(1.0 + (PEER_TOPK - 1 - r) / PEER_TOPK) * TOP16_MARK, work)
        vals_ref[p, r, pl.ds(hh, 1), :] = m
    marked = work <= -TOP16_MARK
    rank = jnp.where(marked, (2.0 * PEER_TOPK - 1.0) + work * (PEER_TOPK / TOP16_MARK), float(PEER_TOPK))
    n_marked = jnp.sum(jnp.where(marked, 1.0, 0.0), axis=0, keepdims=True)
    ok = jnp.max(jnp.abs(n_marked - float(PEER_TOPK))) == 0.0
    return rank, ok


def _routed_chunk(a_ref, p_ref, chunk, rank2_ref, e2_ref, n_ref, e1_ref):
    ec, tb = a_ref.shape
    rows_per_tile = 16
    for g in range(ec // PEER_NKEYS):
        i = chunk * (ec // PEER_NKEYS) + g
        n_rows = [n_ref[hh, pl.ds(i, 1), :] for hh in range(PEER_HEADS)]
        e1_rows = [e1_ref[hh, pl.ds(i, 1), :] for hh in range(PEER_HEADS)]
        for lt in range(tb // LANES):
            cols = slice(lt * LANES, (lt + 1) * LANES)
            n_b = [jnp.broadcast_to(n_rows[hh][:, cols], (rows_per_tile, LANES)).astype(ROUTE_DTYPE)
                   for hh in range(PEER_HEADS)]
            e1_b = [jnp.broadcast_to(e1_rows[hh][:, cols], (rows_per_tile, LANES)).astype(ROUTE_DTYPE)
                    for hh in range(PEER_HEADS)]
            for sg in range(PEER_NKEYS // rows_per_tile):
                rows = slice(sg * rows_per_tile, (sg + 1) * rows_per_tile)
                wgt = None
                for hh in range(PEER_HEADS):
                    term = jnp.where(rank2_ref[hh, rows, cols] < n_b[hh], e2_ref[hh, rows, cols],
                                     jnp.zeros((), ROUTE_DTYPE)) * e1_b[hh]
                    wgt = term if wgt is None else wgt + term
                arows = slice(g * PEER_NKEYS + sg * rows_per_tile, g * PEER_NKEYS + (sg + 1) * rows_per_tile)
                act = _gelu_sigmoid_form(a_ref[arows, cols])
                p_ref[arows, cols] = (wgt * act.astype(ROUTE_DTYPE)).astype(p_ref.dtype)


def _peer_kernel(x_ref, g_ref, wq_ref, sk_ref, u_ref, vt_ref, gf_ref, o_ref,
                 ht_ref, yt_ref, q_ref, rank1_ref, rank2_ref, e1_ref, e2_ref, n_ref,
                 vals_ref, na_ref, a0_ref, a1_ref, p0_ref, p1_ref, *, final_norm):
    k = pl.program_id(1)
    tb = x_ref.shape[0]
    ec = a0_ref.shape[0]

    @pl.when(k == 0)
    def _route():
        h = _rms(x_ref[...], g_ref[...])
        ht = h.T.astype(MXU_DTYPE)
        ht_ref[...] = ht
        q_ref[...] = jnp.dot(wq_ref[...], ht, preferred_element_type=jnp.float32).astype(MXU_DTYPE)
        yt_ref[...] = jnp.zeros_like(yt_ref)
        iota_f = lax.broadcasted_iota(jnp.int32, (PEER_NKEYS, tb), 0).astype(jnp.float32)

        def head_scores(hh, carry):
            scores = []
            all_ok = None
            for p in range(2):
                idx = hh * 2 + p
                start = pl.multiple_of(idx * PEER_NKEYS, PEER_NKEYS)
                qhp = q_ref[pl.ds(start, PEER_NKEYS), :]
                s = jnp.dot(sk_ref[idx], qhp.astype(MXU_DTYPE), preferred_element_type=jnp.float32)
                s = s + 0.0
                scores.append(s)
                rank, ok = _top16_no_ties(s, vals_ref, p, hh)
                all_ok = ok if all_ok is None else jnp.logical_and(all_ok, ok)
                e = jnp.exp(s - vals_ref[p, 0, pl.ds(hh, 1), :])
                if p == 0:
                    rank1_ref[hh] = rank
                    e1_ref[hh] = e
                else:
                    rank2_ref[hh] = rank.astype(ROUTE_DTYPE)
                    e2_ref[hh] = e.astype(ROUTE_DTYPE)

            @pl.when(jnp.logical_not(all_ok))
            def _exact():
                rank1_ref[hh] = _top16(scores[0], iota_f, vals_ref, 0, hh)
                rank2_ref[hh] = _top16(scores[1], iota_f, vals_ref, 1, hh).astype(ROUTE_DTYPE)
            return carry

        lax.fori_loop(0, PEER_HEADS, head_scores, 0)

        pairs = _candidate_pairs()
        v1 = [vals_ref[0, a] for a in range(PEER_TOPK)]
        v2 = [vals_ref[1, b] for b in range(PEER_TOPK)]
        cand = {ab: v1[ab[0]] + v2[ab[1]] for ab in pairs}
        pos = {ab: jnp.full((PEER_HEADS, tb), float((ab[0] + 1) * (ab[1] + 1) - 1), jnp.float32) for ab in pairs}
        for i, x_ab in enumerate(pairs):
            for y_ab in pairs[i + 1:]:
                (a, b), (a2, b2) = x_ab, y_ab
                if (a2 >= a and b2 >= b) or (a2 <= a and b2 <= b):
                    continue
                x_first = cand[x_ab] >= cand[y_ab]
                pos[y_ab] = pos[y_ab] + jnp.where(x_first, 1.0, 0.0)
                pos[x_ab] = pos[x_ab] + jnp.where(x_first, 0.0, 1.0)
        c00 = cand[(0, 0)]
        z = jnp.zeros((PEER_HEADS, tb), jnp.float32)
        n_a = [jnp.zeros((PEER_HEADS, tb), jnp.float32) for _ in range(PEER_TOPK)]
        for (a, b) in pairs:
            chosen = pos[(a, b)] < float(PEER_TOPK)
            n_a[a] = n_a[a] + jnp.where(chosen, 1.0, 0.0)
            z = z + jnp.where(chosen, jnp.exp(cand[(a, b)] - c00), 0.0)
        inv_z = 1.0 / z
        for a in range(PEER_TOPK):
            na_ref[a] = n_a[a]
        na_ref[PEER_TOPK] = inv_z

        def head_rows(hh, carry):
            rank1 = rank1_ref[hh]
            n_i = jnp.zeros((PEER_NKEYS, tb), jnp.float32)
            for a in range(PEER_TOPK):
                n_i = jnp.where(rank1 == float(a), na_ref[a, pl.ds(hh, 1), :], n_i)
            n_ref[hh] = n_i
            e1_ref[hh] = e1_ref[hh] * na_ref[PEER_TOPK, pl.ds(hh, 1), :]
            return carry

        lax.fori_loop(0, PEER_HEADS, head_rows, 0)

    routing = (rank2_ref, e2_ref, n_ref, e1_ref)
    last = pl.num_programs(1) - 1

    def stage1(a_ref, half):
        a_ref[...] = jnp.dot(u_ref[half * ec:(half + 1) * ec, :], ht_ref[...], preferred_element_type=jnp.float32)

    def stage3(p_ref, half):
        return jnp.dot(vt_ref[:, half * ec:(half + 1) * ec], p_ref[...], preferred_element_type=jnp.float32)

    @pl.when(k == 0)
    def _fill():
        stage1(a0_ref, 0)
        stage1(a1_ref, 1)
        _routed_chunk(a0_ref, p0_ref, 0, *routing)

    @pl.when(jnp.logical_and(k > 0, k < last))
    def _steady():
        y_even = stage3(p0_ref, 0)
        stage1(a0_ref, 0)
        _routed_chunk(a1_ref, p1_ref, 2 * k - 1, *routing)
        y_odd = stage3(p1_ref, 1)
        stage1(a1_ref, 1)
        _routed_chunk(a0_ref, p0_ref, 2 * k, *routing)
        yt_ref[...] += y_even + y_odd

    @pl.when(k == last)
    def _drain():
        y_even = stage3(p0_ref, 0)
        _routed_chunk(a1_ref, p1_ref, 2 * k - 1, *routing)
        yt = yt_ref[...] + y_even + stage3(p1_ref, 1)
        out = x_ref[...] + yt.T
        if final_norm:
            out = _rms(out, gf_ref[...])
        o_ref[...] = out


def _peer(x, g, w_q, subkeys, u_tab, v_tab, g_final, final_norm):
    T, D = x.shape
    tb = min(PEER_TB, T)
    ec = PEER_EC
    assert T % tb == 0 and PEER_N_EXPERTS % (2 * ec) == 0 and ec % PEER_NKEYS == 0
    n_pair = PEER_N_EXPERTS // (2 * ec)
    hq = PEER_HEADS * PEER_DKEY
    wq_t = w_q.T.astype(MXU_DTYPE)
    sk = subkeys.reshape(PEER_HEADS * 2, PEER_NKEYS, PEER_DKEY // 2).astype(MXU_DTYPE)
    u_b = u_tab.astype(MXU_DTYPE)
    vt_b = v_tab.T.astype(MXU_DTYPE)
    head_tile = (PEER_HEADS, PEER_NKEYS, tb)
    return pl.pallas_call(
        functools.partial(_peer_kernel, final_norm=final_norm),
        grid=(T // tb, n_pair + 1),
        in_specs=[
            pl.BlockSpec((tb, D), lambda t, k: (t, 0)),
            pl.BlockSpec((1, D), lambda t, k: (0, 0)),
            pl.BlockSpec((hq, D), lambda t, k: (0, 0), pipeline_mode=pl.Buffered(1)),
            pl.BlockSpec((PEER_HEADS * 2, PEER_NKEYS, PEER_DKEY // 2), lambda t, k: (0, 0, 0)),
            pl.BlockSpec((2 * ec, D), lambda t, k: (jnp.minimum(k, n_pair - 1), 0)),
            pl.BlockSpec((D, 2 * ec), lambda t, k: (0, jnp.maximum(k - 1, 0))),
            pl.BlockSpec((1, D), lambda t, k: (0, 0)),
        ],
        out_specs=pl.BlockSpec((tb, D), lambda t, k: (t, 0)),
        out_shape=jax.ShapeDtypeStruct((T, D), jnp.float32),
        scratch_shapes=[
            pltpu.VMEM((D, tb), MXU_DTYPE),
            pltpu.VMEM((D, tb), jnp.float32),
            pltpu.VMEM((hq, tb), MXU_DTYPE),
            pltpu.VMEM(head_tile, jnp.float32),
            pltpu.VMEM(head_tile, ROUTE_DTYPE),
            pltpu.VMEM(head_tile, jnp.float32),
            pltpu.VMEM(head_tile, ROUTE_DTYPE),
            pltpu.VMEM(head_tile, jnp.float32),
            pltpu.VMEM((2, PEER_TOPK, PEER_HEADS, tb), jnp.float32),
            pltpu.VMEM((PEER_TOPK + 1, PEER_HEADS, tb), jnp.float32),
            pltpu.VMEM((ec, tb), jnp.float32),
            pltpu.VMEM((ec, tb), jnp.float32),
            pltpu.VMEM((ec, tb), MXU_DTYPE),
            pltpu.VMEM((ec, tb), MXU_DTYPE),
        ],
        compiler_params=pltpu.CompilerParams(dimension_semantics=("arbitrary", "arbitrary"),
                                             vmem_limit_bytes=VMEM_LIMIT),
        name="peer_final" if final_norm else "peer",
    )(x, g.reshape(1, D), wq_t, sk, u_b, vt_b, g_final.reshape(1, D))


def _qkv_kernel(x_ref, g_ref, w_ref, q_ref, k_ref, v_ref):
    h = _rms(x_ref[...], g_ref[...])
    qkv = _mm(h, w_ref[...])
    d = q_ref.shape[1]
    q_ref[...] = qkv[:, :d]
    k_ref[...] = qkv[:, d:2 * d]
    v_ref[...] = qkv[:, 2 * d:]


def _qkv(x, g, w_qkv):
    T, D = x.shape
    tm = min(MM_TM, T)
    aw = MOBA_HEADS * MOBA_HEAD_DIM
    out = jax.ShapeDtypeStruct((T, aw), jnp.float32)
    return pl.pallas_call(
        _qkv_kernel,
        grid=(T // tm,),
        in_specs=[pl.BlockSpec((tm, D), lambda i: (i, 0)),
                  pl.BlockSpec((1, D), lambda i: (0, 0)),
                  pl.BlockSpec((D, 3 * aw), lambda i: (0, 0))],
        out_specs=[pl.BlockSpec((tm, aw), lambda i: (i, 0))] * 3,
        out_shape=[out, out, out],
        compiler_params=pltpu.CompilerParams(dimension_semantics=("arbitrary",),
                                             vmem_limit_bytes=VMEM_LIMIT),
        name="qkv",
    )(x, g.reshape(1, D), w_qkv.astype(MXU_DTYPE))


def _oproj_kernel(x_ref, o_ref, w_ref, y_ref):
    y_ref[...] = x_ref[...] + _mm(o_ref[...], w_ref[...])


def _oproj(x, o, w_o):
    T, D = x.shape
    tm = min(MM_TM, T)
    aw = o.shape[1]
    return pl.pallas_call(
        _oproj_kernel,
        grid=(T // tm,),
        in_specs=[pl.BlockSpec((tm, D), lambda i: (i, 0)),
                  pl.BlockSpec((tm, aw), lambda i: (i, 0)),
                  pl.BlockSpec((aw, D), lambda i: (0, 0))],
        out_specs=pl.BlockSpec((tm, D), lambda i: (i, 0)),
        out_shape=jax.ShapeDtypeStruct((T, D), jnp.float32),
        compiler_params=pltpu.CompilerParams(dimension_semantics=("arbitrary",),
                                             vmem_limit_bytes=VMEM_LIMIT),
        name="oproj",
    )(x, o, w_o.astype(MXU_DTYPE))


def _t5_bucket(dist):
    n = jnp.maximum(dist, 0)
    max_exact = REL_BUCKETS // 2
    nf = jnp.maximum(n, 1).astype(jnp.float32)
    log_b = max_exact + (jnp.log(nf / max_exact) / math.log(REL_MAX_DIST / max_exact)
                         * (REL_BUCKETS - max_exact)).astype(jnp.int32)
    return jnp.where(n < max_exact, n, jnp.minimum(log_b, REL_BUCKETS - 1))


def _bias_kernel(rb_ref, o_ref):
    h = pl.program_id(0)
    dblk = pl.program_id(1)
    key = lax.broadcasted_iota(jnp.int32, (MOBA_BLOCK, MOBA_BLOCK), 0)
    qry = lax.broadcasted_iota(jnp.int32, (MOBA_BLOCK, MOBA_BLOCK), 1)
    dist = dblk * MOBA_BLOCK + qry - key
    bucket = _t5_bucket(dist)
    val = jnp.zeros((MOBA_BLOCK, MOBA_BLOCK), jnp.float32)
    for b in range(REL_BUCKETS):
        val = jnp.where(bucket == b, rb_ref[b, h], val)
    o_ref[0, 0] = jnp.where(dist >= 0, val * LOG2E, NEG)


def _bias_tiles(rel_bias):
    last_tile_min_dist = (BIAS_TILES - 1) * MOBA_BLOCK - (MOBA_BLOCK - 1)
    max_exact = REL_BUCKETS // 2
    first_last_bucket = max_exact * (REL_MAX_DIST / max_exact) ** ((REL_BUCKETS - 1 - max_exact) / (REL_BUCKETS - max_exact))
    assert last_tile_min_dist > first_last_bucket + 1
    return pl.pallas_call(
        _bias_kernel,
        grid=(MOBA_HEADS, BIAS_TILES),
        in_specs=[pl.BlockSpec(memory_space=pltpu.SMEM)],
        out_specs=pl.BlockSpec((1, 1, MOBA_BLOCK, MOBA_BLOCK), lambda h, d: (h, d, 0, 0)),
        out_shape=jax.ShapeDtypeStruct((MOBA_HEADS, BIAS_TILES, MOBA_BLOCK, MOBA_BLOCK), jnp.float32),
        compiler_params=pltpu.CompilerParams(dimension_semantics=("arbitrary", "arbitrary")),
        name="t5_bias_tiles",
    )(rel_bias)


def _attn_kernel(qa_ref, qb_ref, k_ref, v_ref, bias_ref, o_ref,
                 kb_ref, vt_ref, kmean_ref, qs_ref, sel_ref, m_ref, l_ref, acc_ref, s_ref):
    j = pl.program_id(2)
    S = k_ref.shape[1]
    n_blk = S // MOBA_BLOCK
    tq = qa_ref.shape[1]

    @pl.when(j == 0)
    def _prep():
        for n in range(n_blk):
            kblk = k_ref[0, n * MOBA_BLOCK:(n + 1) * MOBA_BLOCK, :]
            kb_ref[n * MOBA_BLOCK:(n + 1) * MOBA_BLOCK, :] = kblk.astype(MXU_DTYPE)
            kmean_ref[n:n + 1, :] = jnp.mean(kblk, axis=0, keepdims=True)
            vt_ref[:, n * MOBA_BLOCK:(n + 1) * MOBA_BLOCK] = (
                v_ref[0, n * MOBA_BLOCK:(n + 1) * MOBA_BLOCK, :].T.astype(MXU_DTYPE))

    owns = (j, n_blk - 1 - j)
    low_head = lax.broadcasted_iota(jnp.int32, (tq, LANES), 1) < MOBA_HEAD_DIM
    blk = lax.broadcasted_iota(jnp.int32, (n_blk, tq), 0)
    for t, q_ref in enumerate((qa_ref, qb_ref)):
        q = q_ref[0]
        past = blk < owns[t]
        for e in range(2):
            q_e = jnp.where(low_head, q, 0.0) if e == 0 else jnp.where(low_head, 0.0, q)
            gate = jnp.where(past, _mm_nt(kmean_ref[...], q_e), -jnp.inf)
            beaten = jnp.zeros((n_blk, tq), jnp.float32)
            for m in range(n_blk):
                gm = gate[m:m + 1, :]
                beats = (gm > gate) | ((gm == gate) & (blk > m))
                beaten = beaten + jnp.where(beats, 1.0, 0.0)
            chosen = (past & (beaten < float(MOBA_TOPK))) | (blk == owns[t])
            sel_ref[t, e] = jnp.where(chosen, 1.0, 0.0)
            qs_ref[t, e] = (q_e * (MOBA_HEAD_DIM ** -0.5 * LOG2E)).astype(MXU_DTYPE)
            m_ref[t, e] = jnp.full((1, tq), NEG, jnp.float32)
            l_ref[t, e] = jnp.zeros((1, tq), jnp.float32)
            acc_ref[t, e] = jnp.zeros((LANES, tq), jnp.float32)

    def unit(u):
        is_a = u <= j
        t = jnp.where(is_a, 0, 1)
        n = jnp.where(is_a, j - u, n_blk - u)
        own = jnp.where(is_a, j, n_blk - 1 - j)
        return t, n, jnp.minimum(own - n, BIAS_TILES - 1)

    for u in range(n_blk + 1):
        t, n, tile = unit(u)
        k_n = kb_ref[pl.ds(pl.multiple_of(n * MOBA_BLOCK, MOBA_BLOCK), MOBA_BLOCK), :]
        for e in range(2):
            st = _mm_nt(k_n, qs_ref[t, e]) + bias_ref[e, tile]
            st = jnp.where(sel_ref[t, e, pl.ds(n, 1), :] > 0.0, st, NEG)
            s_ref[u, e] = st
            m_ref[t, e] = jnp.maximum(m_ref[t, e], jnp.max(st, axis=0, keepdims=True))

    for u in range(n_blk + 1):
        t, n, _ = unit(u)
        vt_n = vt_ref[:, pl.ds(pl.multiple_of(n * MOBA_BLOCK, MOBA_BLOCK), MOBA_BLOCK)]
        for e in range(2):
            p = jnp.exp2(s_ref[u, e] - m_ref[t, e])
            l_ref[t, e] = l_ref[t, e] + jnp.sum(p, axis=0, keepdims=True)
            acc_ref[t, e] = acc_ref[t, e] + jnp.dot(vt_n, p.astype(MXU_DTYPE), preferred_element_type=jnp.float32)

    for t in range(2):
        o0 = acc_ref[t, 0] / l_ref[t, 0]
        o1 = acc_ref[t, 1] / l_ref[t, 1]
        ot = jnp.concatenate([o0[:MOBA_HEAD_DIM], o1[MOBA_HEAD_DIM:]], axis=0)
        row0 = pl.multiple_of(owns[t] * tq, tq)
        o_ref[0, pl.ds(row0, tq), :] = ot.T


def _attention(q, k, v, bias_tiles):
    B, S, aw = q.shape
    tq = ATT_TQ
    assert S % (2 * MOBA_BLOCK) == 0 and tq == MOBA_BLOCK
    n_blk = S // MOBA_BLOCK
    n_pair = aw // LANES
    return pl.pallas_call(
        _attn_kernel,
        grid=(n_pair, B, n_blk // 2),
        in_specs=[
            pl.BlockSpec((1, tq, LANES), lambda hp, b, j: (b, j, hp)),
            pl.BlockSpec((1, tq, LANES), lambda hp, b, j: (b, n_blk - 1 - j, hp)),
            pl.BlockSpec((1, S, LANES), lambda hp, b, j: (b, 0, hp)),
            pl.BlockSpec((1, S, LANES), lambda hp, b, j: (b, 0, hp)),
            pl.BlockSpec((2, BIAS_TILES, MOBA_BLOCK, MOBA_BLOCK), lambda hp, b, j: (hp, 0, 0, 0)),
        ],
        out_specs=pl.BlockSpec((1, S, LANES), lambda hp, b, j: (b, 0, hp)),
        out_shape=jax.ShapeDtypeStruct((B, S, aw), jnp.float32),
        scratch_shapes=[
            pltpu.VMEM((S, LANES), MXU_DTYPE),
            pltpu.VMEM((LANES, S), MXU_DTYPE),
            pltpu.VMEM((n_blk, LANES), jnp.float32),
            pltpu.VMEM((2, 2, tq, LANES), MXU_DTYPE),
            pltpu.VMEM((2, 2, n_blk, tq), jnp.float32),
            pltpu.VMEM((2, 2, 1, tq), jnp.float32),
            pltpu.VMEM((2, 2, 1, tq), jnp.float32),
            pltpu.VMEM((2, 2, LANES, tq), jnp.float32),
            pltpu.VMEM((n_blk + 1, 2, MOBA_BLOCK, tq), jnp.float32),
        ],
        compiler_params=pltpu.CompilerParams(dimension_semantics=("arbitrary", "arbitrary", "arbitrary"),
                                             vmem_limit_bytes=VMEM_LIMIT),
        name="moba_attention",
    )(q, q, k, v, bias_tiles)


def kernel(x, ev_w_in, ev_conv_w, ev_conv_b, ev_conv_ln_g, ev_conv_ln_b, ev_sgu_ln_g, ev_sgu_ln_b, ev_sgu_w, ev_sgu_b, ev_w_out, od_w_qkv, od_w_o, rel_bias, peer_w_q, peer_subkeys, peer_u, peer_v, norm_mix_g, norm_ffn_g, norm_final_g):
    B, S, D = x.shape
    T = B * S
    x = _mixer(x, norm_mix_g[0], ev_w_in[0], ev_conv_w[0], ev_conv_b[0], ev_conv_ln_g[0], ev_conv_ln_b[0],
               ev_sgu_ln_g[0], ev_sgu_ln_b[0], ev_sgu_w[0], ev_sgu_b[0], ev_w_out[0])
    xt = _peer(x.reshape(T, D), norm_ffn_g[0], peer_w_q[0], peer_subkeys[0], peer_u[0], peer_v[0],
               norm_final_g, final_norm=False)
    q, k, v = _qkv(xt, norm_mix_g[1], od_w_qkv[0])
    aw = MOBA_HEADS * MOBA_HEAD_DIM
    o = _attention(q.reshape(B, S, aw), k.reshape(B, S, aw), v.reshape(B, S, aw), _bias_tiles(rel_bias))
    xt = _oproj(xt, o.reshape(T, aw), od_w_o[0])
    xt = _peer(xt, norm_ffn_g[1], peer_w_q[1], peer_subkeys[1], peer_u[1], peer_v[1],
               norm_final_g, final_norm=True)
    return xt.reshape(B, S, D)
```

```python
import functools
import math

import jax
import jax.numpy as jnp
from jax import lax
from jax.experimental import pallas as pl
from jax.experimental.pallas import tpu as pltpu

D_MODEL = 1024
A_CH = 512
CONV_W = 31
B_HEADS = 8
B_HEAD_DIM = 64
B_CH = B_HEADS * B_HEAD_DIM
SGU_CHUNK = 128
MIX_IN = 2 * A_CH + 2 * B_CH
MOBA_HEADS = 16
MOBA_HEAD_DIM = 64
MOBA_BLOCK = 256
MOBA_TOPK = 3
REL_BUCKETS = 32
REL_MAX_DIST = 2048
PEER_HEADS = 8
PEER_NKEYS = 128
PEER_N_EXPERTS = PEER_NKEYS * PEER_NKEYS
PEER_DKEY = 256
PEER_TOPK = 16
EPS = 1e-6
NEG = -1e30
LOG2E = math.log2(math.e)

LANES = 128
MXU_DTYPE = jnp.bfloat16
ROUTE_DTYPE = jnp.bfloat16
VMEM_LIMIT = 56 * 1024 * 1024

MIX_TS = 512
CONV_HALO = 32
PEER_TB = 512
PEER_EC = 1024
MM_TM = 512
ATT_TQ = MOBA_BLOCK
BIAS_TILES = 8


def _rms(x, g):
    return x * lax.rsqrt(jnp.mean(x * x, axis=-1, keepdims=True) + EPS) * g


def _layer_norm(x, g, b):
    mu = jnp.mean(x, axis=-1, keepdims=True)
    var = jnp.mean(jnp.square(x - mu), axis=-1, keepdims=True)
    return (x - mu) * lax.rsqrt(var + EPS) * g + b


def _gelu(x):
    cdf = 0.5 * (1.0 + jnp.tanh(math.sqrt(2.0 / math.pi) * (x + 0.044715 * (x * x * x))))
    return x * cdf


def _mm(a, b):
    return jnp.dot(a.astype(MXU_DTYPE), b.astype(MXU_DTYPE), preferred_element_type=jnp.float32)


def _mm_nt(a, b):
    return lax.dot_general(a.astype(MXU_DTYPE), b.astype(MXU_DTYPE), (((1,), (1,)), ((), ())),
                           preferred_element_type=jnp.float32)


def _mixer_kernel(x_ref, g_ref, win_ref, cw_ref, cb_ref, clg_ref, clb_ref, slg_ref, slb_ref,
                  sw_ref, sb_ref, wout_ref, o_ref, abuf_ref, s_ref, shift0_ref, shift1_ref):
    ts = x_ref.shape[1]
    x = x_ref[0]
    h = _rms(x, g_ref[...])
    z = _mm(h, win_ref[...])
    a = z[:, :A_CH] * jax.nn.sigmoid(z[:, A_CH:2 * A_CH])

    @pl.when(pl.program_id(1) == 0)
    def _():
        abuf_ref[0:CONV_HALO, :] = jnp.zeros((CONV_HALO, A_CH), jnp.float32)

    @pl.when(pl.program_id(1) > 0)
    def _():
        abuf_ref[0:CONV_HALO, :] = abuf_ref[ts:ts + CONV_HALO, :]

    abuf_ref[CONV_HALO:CONV_HALO + ts, :] = a
    acc = jnp.broadcast_to(cb_ref[...], (ts, A_CH))
    first = CONV_HALO - (CONV_W - 1)
    for r in range(8):
        taps = [w for w in range(CONV_W) if (first + w) % 8 == r]
        span = max((first + w) // 8 for w in taps) * 8 + ts
        shifted_ref = (shift0_ref, shift1_ref)[r % 2]
        shifted_ref[0:span, :] = abuf_ref[r:r + span, :]
        for w in taps:
            base = ((first + w) // 8) * 8
            acc = acc + cw_ref[w:w + 1, :] * shifted_ref[base:base + ts, :]
    a_n = _layer_norm(acc, clg_ref[...], clb_ref[...])
    a_out = a_n * jax.nn.sigmoid(a_n)

    u = _gelu(z[:, 2 * A_CH:2 * A_CH + B_CH])
    v = _layer_norm(_gelu(z[:, 2 * A_CH + B_CH:]), slg_ref[...], slb_ref[...])
    row = lax.broadcasted_iota(jnp.int32, (SGU_CHUNK, SGU_CHUNK), 0)
    col = lax.broadcasted_iota(jnp.int32, (SGU_CHUNK, SGU_CHUNK), 1)
    low_head = lax.broadcasted_iota(jnp.int32, (SGU_CHUNK, LANES), 1) < B_HEAD_DIM
    wm = [jnp.where(col <= row, sw_ref[hh], 0.0).astype(MXU_DTYPE) for hh in range(B_HEADS)]
    for c in range(ts // SGU_CHUNK):
        for pr in range(B_HEADS // 2):
            vp = v[c * SGU_CHUNK:(c + 1) * SGU_CHUNK, pr * LANES:(pr + 1) * LANES].astype(MXU_DTYPE)
            s0 = jnp.dot(wm[2 * pr], vp, preferred_element_type=jnp.float32)
            s1 = jnp.dot(wm[2 * pr + 1], vp, preferred_element_type=jnp.float32)
            s_ref[c * SGU_CHUNK:(c + 1) * SGU_CHUNK, pr * LANES:(pr + 1) * LANES] = (
                jnp.where(low_head, s0, s1) + sb_ref[:, pr * LANES:(pr + 1) * LANES])
    bo = u * s_ref[...]
    out = _mm(a_out, wout_ref[0:A_CH, :]) + _mm(bo, wout_ref[A_CH:, :])
    o_ref[0] = x + out


def _mixer(x, g, w_in, conv_w, conv_b, cln_g, cln_b, sln_g, sln_b, sgu_w, sgu_b, w_out):
    B, S, D = x.shape
    ts = min(MIX_TS, S)
    assert S % ts == 0 and ts % SGU_CHUNK == 0
    row2 = lambda a: a.reshape(1, -1)
    sb_exp = jnp.repeat(sgu_b.T, B_HEAD_DIM, axis=1)
    const = lambda shape: pl.BlockSpec(shape, lambda b, s: (0,) * len(shape))
    return pl.pallas_call(
        _mixer_kernel,
        grid=(B, S // ts),
        in_specs=[
            pl.BlockSpec((1, ts, D), lambda b, s: (b, s, 0)),
            const((1, D)), const((D, MIX_IN)), const((CONV_W, A_CH)), const((1, A_CH)),
            const((1, A_CH)), const((1, A_CH)), const((1, B_CH)), const((1, B_CH)),
            const((B_HEADS, SGU_CHUNK, SGU_CHUNK)), const((SGU_CHUNK, B_CH)), const((A_CH + B_CH, D)),
        ],
        out_specs=pl.BlockSpec((1, ts, D), lambda b, s: (b, s, 0)),
        out_shape=jax.ShapeDtypeStruct((B, S, D), jnp.float32),
        scratch_shapes=[pltpu.VMEM((ts + CONV_HALO, A_CH), jnp.float32),
                        pltpu.VMEM((ts, B_CH), jnp.float32),
                        pltpu.VMEM((ts + CONV_HALO, A_CH), jnp.float32),
                        pltpu.VMEM((ts + CONV_HALO, A_CH), jnp.float32)],
        compiler_params=pltpu.CompilerParams(dimension_semantics=("arbitrary", "arbitrary"),
                                             vmem_limit_bytes=VMEM_LIMIT),
        name="mixer0",
    )(x, row2(g), w_in.astype(MXU_DTYPE), conv_w, row2(conv_b), row2(cln_g), row2(cln_b),
      row2(sln_g), row2(sln_b), sgu_w, sb_exp, w_out.astype(MXU_DTYPE))


def _candidate_pairs():
    return [(a, b) for a in range(PEER_TOPK) for b in range(PEER_TOPK) if (a + 1) * (b + 1) <= PEER_TOPK]


def _top16(s, iota_f, vals_ref, p, hh):
    work = s
    rank = jnp.full(s.shape, float(PEER_TOPK), jnp.float32)
    for r in range(PEER_TOPK):
        m = jnp.max(work, axis=0, keepdims=True)
        first = jnp.min(jnp.where(work == m, iota_f, float(PEER_NKEYS)), axis=0, keepdims=True)
        sel = iota_f == first
        rank = jnp.where(sel, float(r), rank)
        work = jnp.where(sel, -jnp.inf, work)
        vals_ref[p, r, pl.ds(hh, 1), :] = m
    return rank


def _gelu_sigmoid_form(x):
    k0 = -2.0 * math.sqrt(2.0 / math.pi) * math.log2(math.e)
    u = x * (k0 + (k0 * 0.044715) * (x * x))
    return x / (1.0 + jnp.exp2(u))


TOP16_MARK = 2.0 ** 127


def _top16_no_ties(s, vals_ref, p, hh):
    work = s
    for r in range(PEER_TOPK):
        m = jnp.max(work, axis=0, keepdims=True)
        work = jnp.where(work == m, -(1.0 + (PEER_TOPK - 1 - r) / PEER_TOPK) * TOP16_MARK, work)
        vals_ref[p, r, pl.ds(hh, 1), :] = m
    marked = work <= -TOP16_MARK
    rank = jnp.where(marked, (2.0 * PEER_TOPK - 1.0) + work * (PEER_TOPK / TOP16_MARK), float(PEER_TOPK))
    n_marked = jnp.sum(jnp.where(marked, 1.0, 0.0), axis=0, keepdims=True)
    ok = jnp.max(jnp.abs(n_marked - float(PEER_TOPK))) == 0.0
    return rank, ok


def _routed_chunk(a_ref, p_ref, chunk, rank2_ref, e2_ref, n_ref, e1_ref):
    ec, tb = a_ref.shape
    rows_per_tile = 16
    for g in range(ec // PEER_NKEYS):
        i = chunk * (ec // PEER_NKEYS) + g
        n_rows = [n_ref[hh, pl.ds(i, 1), :] for hh in range(PEER_HEADS)]
        e1_rows = [e1_ref[hh, pl.ds(i, 1), :] for hh in range(PEER_HEADS)]
        for lt in range(tb // LANES):
            cols = slice(lt * LANES, (lt + 1) * LANES)
            n_b = [jnp.broadcast_to(n_rows[hh][:, cols], (rows_per_tile, LANES)).astype(ROUTE_DTYPE)
                   for hh in range(PEER_HEADS)]
            e1_b = [jnp.broadcast_to(e1_rows[hh][:, cols], (rows_per_tile, LANES)).astype(ROUTE_DTYPE)
                    for hh in range(PEER_HEADS)]
            for sg in range(PEER_NKEYS // rows_per_tile):
                rows = slice(sg * rows_per_tile, (sg + 1) * rows_per_tile)
                wgt = None
                for hh in range(PEER_HEADS):
                    term = jnp.where(rank2_ref[hh, rows, cols] < n_b[hh], e2_ref[hh, rows, cols],
                                     jnp.zeros((), ROUTE_DTYPE)) * e1_b[hh]
                    wgt = term if wgt is None else wgt + term
                arows = slice(g * PEER_NKEYS + sg * rows_per_tile, g * PEER_NKEYS + (sg + 1) * rows_per_tile)
                act = _gelu_sigmoid_form(a_ref[arows, cols])
                p_ref[arows, cols] = (wgt * act.astype(ROUTE_DTYPE)).astype(p_ref.dtype)


def _peer_kernel(x_ref, g_ref, wq_ref, sk_ref, u_ref, vt_ref, gf_ref, o_ref,
                 ht_ref, yt_ref, q_ref, rank1_ref, rank2_ref, e1_ref, e2_ref, n_ref,
                 vals_ref, na_ref, a0_ref, a1_ref, p0_ref, p1_ref, *, final_norm):
    k = pl.program_id(1)
    tb = x_ref.shape[0]
    ec = a0_ref.shape[0]

    @pl.when(k == 0)
    def _route():
        h = _rms(x_ref[...], g_ref[...])
        ht = h.T.astype(MXU_DTYPE)
        ht_ref[...] = ht
        q_ref[...] = jnp.dot(wq_ref[...], ht, preferred_element_type=jnp.float32).astype(MXU_DTYPE)
        yt_ref[...] = jnp.zeros_like(yt_ref)
        iota_f = lax.broadcasted_iota(jnp.int32, (PEER_NKEYS, tb), 0).astype(jnp.float32)

        def head_scores(hh, carry):
            scores = []
            all_ok = None
            for p in range(2):
                idx = hh * 2 + p
                start = pl.multiple_of(idx * PEER_NKEYS, PEER_NKEYS)
                qhp = q_ref[pl.ds(start, PEER_NKEYS), :]
                s = jnp.dot(sk_ref[idx], qhp.astype(MXU_DTYPE), preferred_element_type=jnp.float32)
                s = s + 0.0
                scores.append(s)
                rank, ok = _top16_no_ties(s, vals_ref, p, hh)
                all_ok = ok if all_ok is None else jnp.logical_and(all_ok, ok)
                e = jnp.exp(s - vals_ref[p, 0, pl.ds(hh, 1), :])
                if p == 0:
                    rank1_ref[hh] = rank
                    e1_ref[hh] = e
                else:
                    rank2_ref[hh] = rank.astype(ROUTE_DTYPE)
                    e2_ref[hh] = e.astype(ROUTE_DTYPE)

            @pl.when(jnp.logical_not(all_ok))
            def _exact():
                rank1_ref[hh] = _top16(scores[0], iota_f, vals_ref, 0, hh)
                rank2_ref[hh] = _top16(scores[1], iota_f, vals_ref, 1, hh).astype(ROUTE_DTYPE)
            return carry

        lax.fori_loop(0, PEER_HEADS, head_scores, 0)

        pairs = _candidate_pairs()
        v1 = [vals_ref[0, a] for a in range(PEER_TOPK)]
        v2 = [vals_ref[1, b] for b in range(PEER_TOPK)]
        cand = {ab: v1[ab[0]] + v2[ab[1]] for ab in pairs}
        pos = {ab: jnp.full((PEER_HEADS, tb), float((ab[0] + 1) * (ab[1] + 1) - 1), jnp.float32) for ab in pairs}
        for i, x_ab in enumerate(pairs):
            for y_ab in pairs[i + 1:]:
                (a, b), (a2, b2) = x_ab, y_ab
                if (a2 >= a and b2 >= b) or (a2 <= a and b2 <= b):
                    continue
                x_first = cand[x_ab] >= cand[y_ab]
                pos[y_ab] = pos[y_ab] + jnp.where(x_first, 1.0, 0.0)
                pos[x_ab] = pos[x_ab] + jnp.where(x_first, 0.0, 1.0)
        c00 = cand[(0, 0)]
        z = jnp.zeros((PEER_HEADS, tb), jnp.float32)
        n_a = [jnp.zeros((PEER_HEADS, tb), jnp.float32) for _ in range(PEER_TOPK)]
        for (a, b) in pairs:
            chosen = pos[(a, b)] < float(PEER_TOPK)
            n_a[a] = n_a[a] + jnp.where(chosen, 1.0, 0.0)
            z = z + jnp.where(chosen, jnp.exp(cand[(a, b)] - c00), 0.0)
        inv_z = 1.0 / z
        for a in range(PEER_TOPK):
            na_ref[a] = n_a[a]
        na_ref[PEER_TOPK] = inv_z

        def head_rows(hh, carry):
            rank1 = rank1_ref[hh]
            n_i = jnp.zeros((PEER_NKEYS, tb), jnp.float32)
            for a in range(PEER_TOPK):
                n_i = jnp.where(rank1 == float(a), na_ref[a, pl.ds(hh, 1), :], n_i)
            n_ref[hh] = n_i
            e1_ref[hh] = e1_ref[hh] * na_ref[PEER_TOPK, pl.ds(hh, 1), :]
            return carry

        lax.fori_loop(0, PEER_HEADS, head_rows, 0)

    routing = (rank2_ref, e2_ref, n_ref, e1_ref)
    last = pl.num_programs(1) - 1

    def stage1(a_ref, half):
        a_ref[...] = jnp.dot(u_ref[half * ec:(half + 1) * ec, :], ht_ref[...], preferred_element_type=jnp.float32)

    def stage3(p_ref, half):
        return jnp.dot(vt_ref[:, half * ec:(half + 1) * ec], p_ref[...], preferred_element_type=jnp.float32)

    @pl.when(k == 0)
    def _fill():
        stage1(a0_ref, 0)
        stage1(a1_ref, 1)
        _routed_chunk(a0_ref, p0_ref, 0, *routing)

    @pl.when(jnp.logical_and(k > 0, k < last))
    def _steady():
        y_even = stage3(p0_ref, 0)
        stage1(a0_ref, 0)
        _routed_chunk(a1_ref, p1_ref, 2 * k - 1, *routing)
        y_odd = stage3(p1_ref, 1)
        stage1(a1_ref, 1)
        _routed_chunk(a0_ref, p0_ref, 2 * k, *routing)
        yt_ref[...] += y_even + y_odd

    @pl.when(k == last)
    def _drain():
        y_even = stage3(p0_ref, 0)
        _routed_chunk(a1_ref, p1_ref, 2 * k - 1, *routing)
        yt = yt_ref[...] + y_even + stage3(p1_ref, 1)
        out = x_ref[...] + yt.T
        if final_norm:
            out = _rms(out, gf_ref[...])
        o_ref[...] = out


def _peer(x, g, w_q, subkeys, u_tab, v_tab, g_final, final_norm):
    T, D = x.shape
    tb = min(PEER_TB, T)
    ec = PEER_EC
    assert T % tb == 0 and PEER_N_EXPERTS % (2 * ec) == 0 and ec % PEER_NKEYS == 0
    n_pair = PEER_N_EXPERTS // (2 * ec)
    hq = PEER_HEADS * PEER_DKEY
    wq_t = w_q.T.astype(MXU_DTYPE)
    sk = subkeys.reshape(PEER_HEADS * 2, PEER_NKEYS, PEER_DKEY // 2).astype(MXU_DTYPE)
    u_b = u_tab.astype(MXU_DTYPE)
    vt_b = v_tab.T.astype(MXU_DTYPE)
    head_tile = (PEER_HEADS, PEER_NKEYS, tb)
    return pl.pallas_call(
        functools.partial(_peer_kernel, final_norm=final_norm),
        grid=(T // tb, n_pair + 1),
        in_specs=[
            pl.BlockSpec((tb, D), lambda t, k: (t, 0)),
            pl.BlockSpec((1, D), lambda t, k: (0, 0)),
            pl.BlockSpec((hq, D), lambda t, k: (0, 0), pipeline_mode=pl.Buffered(1)),
            pl.BlockSpec((PEER_HEADS * 2, PEER_NKEYS, PEER_DKEY // 2), lambda t, k: (0, 0, 0)),
            pl.BlockSpec((2 * ec, D), lambda t, k: (jnp.minimum(k, n_pair - 1), 0)),
            pl.BlockSpec((D, 2 * ec), lambda t, k: (0, jnp.maximum(k - 1, 0))),
            pl.BlockSpec((1, D), lambda t, k: (0, 0)),
        ],
        out_specs=pl.BlockSpec((tb, D), lambda t, k: (t, 0)),
        out_shape=jax.ShapeDtypeStruct((T, D), jnp.float32),
        scratch_shapes=[
            pltpu.VMEM((D, tb), MXU_DTYPE),
            pltpu.VMEM((D, tb), jnp.float32),
            pltpu.VMEM((hq, tb), MXU_DTYPE),
            pltpu.VMEM(head_tile, jnp.float32),
            pltpu.VMEM(head_tile, ROUTE_DTYPE),
            pltpu.VMEM(head_tile, jnp.float32),
            pltpu.VMEM(head_tile, ROUTE_DTYPE),
            pltpu.VMEM(head_tile, jnp.float32),
            pltpu.VMEM((2, PEER_TOPK, PEER_HEADS, tb), jnp.float32),
            pltpu.VMEM((PEER_TOPK + 1, PEER_HEADS, tb), jnp.float32),
            pltpu.VMEM((ec, tb), jnp.float32),
            pltpu.VMEM((ec, tb), jnp.float32),
            pltpu.VMEM((ec, tb), MXU_DTYPE),
            pltpu.VMEM((ec, tb), MXU_DTYPE),
        ],
        compiler_params=pltpu.CompilerParams(dimension_semantics=("arbitrary", "arbitrary"),
                                             vmem_limit_bytes=VMEM_LIMIT),
        name="peer_final" if final_norm else "peer",
    )(x, g.reshape(1, D), wq_t, sk, u_b, vt_b, g_final.reshape(1, D))


def _qkv_kernel(x_ref, g_ref, w_ref, q_ref, k_ref, v_ref):
    h = _rms(x_ref[...], g_ref[...])
    qkv = _mm(h, w_ref[...])
    d = q_ref.shape[1]
    q_ref[...] = qkv[:, :d]
    k_ref[...] = qkv[:, d:2 * d]
    v_ref[...] = qkv[:, 2 * d:]


def _qkv(x, g, w_qkv):
    T, D = x.shape
    tm = min(MM_TM, T)
    aw = MOBA_HEADS * MOBA_HEAD_DIM
    out = jax.ShapeDtypeStruct((T, aw), jnp.float32)
    return pl.pallas_call(
        _qkv_kernel,
        grid=(T // tm,),
        in_specs=[pl.BlockSpec((tm, D), lambda i: (i, 0)),
                  pl.BlockSpec((1, D), lambda i: (0, 0)),
                  pl.BlockSpec((D, 3 * aw), lambda i: (0, 0))],
        out_specs=[pl.BlockSpec((tm, aw), lambda i: (i, 0))] * 3,
        out_shape=[out, out, out],
        compiler_params=pltpu.CompilerParams(dimension_semantics=("arbitrary",),
                                             vmem_limit_bytes=VMEM_LIMIT),
        name="qkv",
    )(x, g.reshape(1, D), w_qkv.astype(MXU_DTYPE))


def _oproj_kernel(x_ref, o_ref, w_ref, y_ref):
    y_ref[...] = x_ref[...] + _mm(o_ref[...], w_ref[...])


def _oproj(x, o, w_o):
    T, D = x.shape
    tm = min(MM_TM, T)
    aw = o.shape[1]
    return pl.pallas_call(
        _oproj_kernel,
        grid=(T // tm,),
        in_specs=[pl.BlockSpec((tm, D), lambda i: (i, 0)),
                  pl.BlockSpec((tm, aw), lambda i: (i, 0)),
                  pl.BlockSpec((aw, D), lambda i: (0, 0))],
        out_specs=pl.BlockSpec((tm, D), lambda i: (i, 0)),
        out_shape=jax.ShapeDtypeStruct((T, D), jnp.float32),
        compiler_params=pltpu.CompilerParams(dimension_semantics=("arbitrary",),
                                             vmem_limit_bytes=VMEM_LIMIT),
        name="oproj",
    )(x, o, w_o.astype(MXU_DTYPE))


def _t5_bucket(dist):
    n = jnp.maximum(dist, 0)
    max_exact = REL_BUCKETS // 2
    nf = jnp.maximum(n, 1).astype(jnp.float32)
    log_b = max_exact + (jnp.log(nf / max_exact) / math.log(REL_MAX_DIST / max_exact)
                         * (REL_BUCKETS - max_exact)).astype(jnp.int32)
    return jnp.where(n < max_exact, n, jnp.minimum(log_b, REL_BUCKETS - 1))


def _bias_kernel(rb_ref, o_ref):
    h = pl.program_id(0)
    dblk = pl.program_id(1)
    key = lax.broadcasted_iota(jnp.int32, (MOBA_BLOCK, MOBA_BLOCK), 0)
    qry = lax.broadcasted_iota(jnp.int32, (MOBA_BLOCK, MOBA_BLOCK), 1)
    dist = dblk * MOBA_BLOCK + qry - key
    bucket = _t5_bucket(dist)
    val = jnp.zeros((MOBA_BLOCK, MOBA_BLOCK), jnp.float32)
    for b in range(REL_BUCKETS):
        val = jnp.where(bucket == b, rb_ref[b, h], val)
    o_ref[0, 0] = jnp.where(dist >= 0, val * LOG2E, NEG)


def _bias_tiles(rel_bias):
    last_tile_min_dist = (BIAS_TILES - 1) * MOBA_BLOCK - (MOBA_BLOCK - 1)
    max_exact = REL_BUCKETS // 2
    first_last_bucket = max_exact * (REL_MAX_DIST / max_exact) ** ((REL_BUCKETS - 1 - max_exact) / (REL_BUCKETS - max_exact))
    assert last_tile_min_dist > first_last_bucket + 1
    return pl.pallas_call(
        _bias_kernel,
        grid=(MOBA_HEADS, BIAS_TILES),
        in_specs=[pl.BlockSpec(memory_space=pltpu.SMEM)],
        out_specs=pl.BlockSpec((1, 1, MOBA_BLOCK, MOBA_BLOCK), lambda h, d: (h, d, 0, 0)),
        out_shape=jax.ShapeDtypeStruct((MOBA_HEADS, BIAS_TILES, MOBA_BLOCK, MOBA_BLOCK), jnp.float32),
        compiler_params=pltpu.CompilerParams(dimension_semantics=("arbitrary", "arbitrary")),
        name="t5_bias_tiles",
    )(rel_bias)


def _attn_kernel(qa_ref, qb_ref, k_ref, v_ref, bias_ref, o_ref,
                 kb_ref, vt_ref, kmean_ref, qs_ref, sel_ref, l_ref, acc_ref,
                 m0_ref, m1_ref, s0_ref, s1_ref):
    j = pl.program_id(2)
    S = k_ref.shape[1]
    n_blk = S // MOBA_BLOCK
    n_pairs = n_blk // 2
    tq = qa_ref.shape[1]

    @pl.when(j == 0)
    def _prep():
        for n in range(n_blk):
            kblk = k_ref[0, n * MOBA_BLOCK:(n + 1) * MOBA_BLOCK, :]
            kb_ref[n * MOBA_BLOCK:(n + 1) * MOBA_BLOCK, :] = kblk.astype(MXU_DTYPE)
            kmean_ref[n:n + 1, :] = jnp.mean(kblk, axis=0, keepdims=True)
            vt_ref[:, n * MOBA_BLOCK:(n + 1) * MOBA_BLOCK] = (
                v_ref[0, n * MOBA_BLOCK:(n + 1) * MOBA_BLOCK, :].T.astype(MXU_DTYPE))

    def unit(u, jj):
        is_a = u <= jj
        t = jnp.where(is_a, 0, 1)
        n = jnp.where(is_a, jj - u, n_blk - u)
        own = jnp.where(is_a, jj, n_blk - 1 - jj)
        return t, n, jnp.minimum(own - n, BIAS_TILES - 1)

    def score(m_ref, s_ref):
        owns = (j, n_blk - 1 - j)
        low_head = lax.broadcasted_iota(jnp.int32, (tq, LANES), 1) < MOBA_HEAD_DIM
        blk = lax.broadcasted_iota(jnp.int32, (n_blk, tq), 0)
        for t, q_ref in enumerate((qa_ref, qb_ref)):
            q = q_ref[0]
            past = blk < owns[t]
            for e in range(2):
                q_e = jnp.where(low_head, q, 0.0) if e == 0 else jnp.where(low_head, 0.0, q)
                gate = jnp.where(past, _mm_nt(kmean_ref[...], q_e), -jnp.inf)
                beaten = jnp.zeros((n_blk, tq), jnp.float32)
                for m in range(n_blk):
                    gm = gate[m:m + 1, :]
                    beats = (gm > gate) | ((gm == gate) & (blk > m))
                    beaten = beaten + jnp.where(beats, 1.0, 0.0)
                chosen = (past & (beaten < float(MOBA_TOPK))) | (blk == owns[t])
                sel_ref[t, e] = jnp.where(chosen, 1.0, 0.0)
                qs_ref[t, e] = (q_e * (MOBA_HEAD_DIM ** -0.5 * LOG2E)).astype(MXU_DTYPE)
                m_ref[t, e] = jnp.full((1, tq), NEG, jnp.float32)
        for u in range(n_blk + 1):
            t, n, tile = unit(u, j)
            k_n = kb_ref[pl.ds(pl.multiple_of(n * MOBA_BLOCK, MOBA_BLOCK), MOBA_BLOCK), :]
            for e in range(2):
                st = _mm_nt(k_n, qs_ref[t, e]) + bias_ref[e, tile]
                st = jnp.where(sel_ref[t, e, pl.ds(n, 1), :] > 0.0, st, NEG)
                s_ref[u, e] = st
                m_ref[t, e] = jnp.maximum(m_ref[t, e], jnp.max(st, axis=0, keepdims=True))

    def finish(m_ref, s_ref):
        jp = j - 1
        for t in range(2):
            for e in range(2):
                l_ref[t, e] = jnp.zeros((1, tq), jnp.float32)
                acc_ref[t, e] = jnp.zeros((LANES, tq), jnp.float32)
        for u in range(n_blk + 1):
            t, n, _ = unit(u, jp)
            vt_n = vt_ref[:, pl.ds(pl.multiple_of(n * MOBA_BLOCK, MOBA_BLOCK), MOBA_BLOCK)]
            for e in range(2):
                p = jnp.exp2(s_ref[u, e] - m_ref[t, e])
                l_ref[t, e] = l_ref[t, e] + jnp.sum(p, axis=0, keepdims=True)
                acc_ref[t, e] = acc_ref[t, e] + jnp.dot(vt_n, p.astype(MXU_DTYPE),
                                                        preferred_element_type=jnp.float32)
        for t, own in enumerate((jp, n_blk - 1 - jp)):
            o0 = acc_ref[t, 0] / l_ref[t, 0]
            o1 = acc_ref[t, 1] / l_ref[t, 1]
            ot = jnp.concatenate([o0[:MOBA_HEAD_DIM], o1[MOBA_HEAD_DIM:]], axis=0)
            o_ref[0, pl.ds(pl.multiple_of(own * tq, tq), tq), :] = ot.T

    bufs = ((m0_ref, s0_ref), (m1_ref, s1_ref))

    @pl.when(j == 0)
    def _first():
        score(*bufs[0])

    for par in range(2):
        @pl.when(jnp.logical_and(jnp.logical_and(j > 0, j < n_pairs), j % 2 == par))
        def _steady(par=par):
            score(*bufs[par])
            finish(*bufs[1 - par])

    @pl.when(j == n_pairs)
    def _last():
        finish(*bufs[(n_pairs - 1) % 2])


def _attention(q, k, v, bias_tiles):
    B, S, aw = q.shape
    tq = ATT_TQ
    assert S % (2 * MOBA_BLOCK) == 0 and tq == MOBA_BLOCK
    n_blk = S // MOBA_BLOCK
    n_pairs = n_blk // 2
    n_pair = aw // LANES
    logits = pltpu.VMEM((n_blk + 1, 2, MOBA_BLOCK, tq), jnp.float32)
    maxima = pltpu.VMEM((2, 2, 1, tq), jnp.float32)
    return pl.pallas_call(
        _attn_kernel,
        grid=(n_pair, B, n_pairs + 1),
        in_specs=[
            pl.BlockSpec((1, tq, LANES), lambda hp, b, j: (b, jnp.minimum(j, n_pairs - 1), hp)),
            pl.BlockSpec((1, tq, LANES), lambda hp, b, j: (b, n_blk - 1 - jnp.minimum(j, n_pairs - 1), hp)),
            pl.BlockSpec((1, S, LANES), lambda hp, b, j: (b, 0, hp)),
            pl.BlockSpec((1, S, LANES), lambda hp, b, j: (b, 0, hp)),
            pl.BlockSpec((2, BIAS_TILES, MOBA_BLOCK, MOBA_BLOCK), lambda hp, b, j: (hp, 0, 0, 0)),
        ],
        out_specs=pl.BlockSpec((1, S, LANES), lambda hp, b, j: (b, 0, hp)),
        out_shape=jax.ShapeDtypeStruct((B, S, aw), jnp.float32),
        scratch_shapes=[
            pltpu.VMEM((S, LANES), MXU_DTYPE),
            pltpu.VMEM((LANES, S), MXU_DTYPE),
            pltpu.VMEM((n_blk, LANES), jnp.float32),
            pltpu.VMEM((2, 2, tq, LANES), MXU_DTYPE),
            pltpu.VMEM((2, 2, n_blk, tq), jnp.float32),
            pltpu.VMEM((2, 2, 1, tq), jnp.float32),
            pltpu.VMEM((2, 2, LANES, tq), jnp.float32),
            maxima, maxima,
            logits, logits,
        ],
        compiler_params=pltpu.CompilerParams(dimension_semantics=("arbitrary", "arbitrary", "arbitrary"),
                                             vmem_limit_bytes=VMEM_LIMIT),
        name="moba_attention",
    )(q, q, k, v, bias_tiles)


def kernel(x, ev_w_in, ev_conv_w, ev_conv_b, ev_conv_ln_g, ev_conv_ln_b, ev_sgu_ln_g, ev_sgu_ln_b, ev_sgu_w, ev_sgu_b, ev_w_out, od_w_qkv, od_w_o, rel_bias, peer_w_q, peer_subkeys, peer_u, peer_v, norm_mix_g, norm_ffn_g, norm_final_g):
    B, S, D = x.shape
    T = B * S
    x = _mixer(x, norm_mix_g[0], ev_w_in[0], ev_conv_w[0], ev_conv_b[0], ev_conv_ln_g[0], ev_conv_ln_b[0],
               ev_sgu_ln_g[0], ev_sgu_ln_b[0], ev_sgu_w[0], ev_sgu_b[0], ev_w_out[0])
    xt = _peer(x.reshape(T, D), norm_ffn_g[0], peer_w_q[0], peer_subkeys[0], peer_u[0], peer_v[0],
               norm_final_g, final_norm=False)
    q, k, v = _qkv(xt, norm_mix_g[1], od_w_qkv[0])
    aw = MOBA_HEADS * MOBA_HEAD_DIM
    o = _attention(q.reshape(B, S, aw), k.reshape(B, S, aw), v.reshape(B, S, aw), _bias_tiles(rel_bias))
    xt = _oproj(xt, o.reshape(T, aw), od_w_o[0])
    xt = _peer(xt, norm_ffn_g[1], peer_w_q[1], peer_subkeys[1], peer_u[1], peer_v[1],
               norm_final_g, final_norm=True)
    return xt.reshape(B, S, D)
```

```python
import functools
import math

import jax
import jax.numpy as jnp
from jax import lax
from jax.experimental import pallas as pl
from jax.experimental.pallas import tpu as pltpu

D_MODEL = 1024
A_CH = 512
CONV_W = 31
B_HEADS = 8
B_HEAD_DIM = 64
B_CH = B_HEADS * B_HEAD_DIM
SGU_CHUNK = 128
MIX_IN = 2 * A_CH + 2 * B_CH
MOBA_HEADS = 16
MOBA_HEAD_DIM = 64
MOBA_BLOCK = 256
MOBA_TOPK = 3
REL_BUCKETS = 32
REL_MAX_DIST = 2048
PEER_HEADS = 8
PEER_NKEYS = 128
PEER_N_EXPERTS = PEER_NKEYS * PEER_NKEYS
PEER_DKEY = 256
PEER_TOPK = 16
EPS = 1e-6
NEG = -1e30
LOG2E = math.log2(math.e)

LANES = 128
MXU_DTYPE = jnp.bfloat16
ROUTE_DTYPE = jnp.bfloat16
VMEM_LIMIT = 56 * 1024 * 1024

MIX_TS = 512
CONV_HALO = 32
PEER_TB = 512
PEER_EC = 1024
MM_TM = 512
ATT_TQ = MOBA_BLOCK
BIAS_TILES = 8


def _rms(x, g):
    return x * lax.rsqrt(jnp.mean(x * x, axis=-1, keepdims=True) + EPS) * g


def _layer_norm(x, g, b):
    mu = jnp.mean(x, axis=-1, keepdims=True)
    var = jnp.mean(jnp.square(x - mu), axis=-1, keepdims=True)
    return (x - mu) * lax.rsqrt(var + EPS) * g + b


def _gelu(x):
    cdf = 0.5 * (1.0 + jnp.tanh(math.sqrt(2.0 / math.pi) * (x + 0.044715 * (x * x * x))))
    return x * cdf


def _mm(a, b):
    return jnp.dot(a.astype(MXU_DTYPE), b.astype(MXU_DTYPE), preferred_element_type=jnp.float32)


def _mm_nt(a, b):
    return lax.dot_general(a.astype(MXU_DTYPE), b.astype(MXU_DTYPE), (((1,), (1,)), ((), ())),
                           preferred_element_type=jnp.float32)


def _mixer_kernel(x_ref, g_ref, win_ref, cw_ref, cb_ref, clg_ref, clb_ref, slg_ref, slb_ref,
                  sw_ref, sb_ref, wout_ref, o_ref, abuf_ref, s_ref, shift0_ref, shift1_ref):
    ts = x_ref.shape[1]
    x = x_ref[0]
    h = _rms(x, g_ref[...])
    z = _mm(h, win_ref[...])
    a = z[:, :A_CH] * jax.nn.sigmoid(z[:, A_CH:2 * A_CH])

    @pl.when(pl.program_id(1) == 0)
    def _():
        abuf_ref[0:CONV_HALO, :] = jnp.zeros((CONV_HALO, A_CH), jnp.float32)

    @pl.when(pl.program_id(1) > 0)
    def _():
        abuf_ref[0:CONV_HALO, :] = abuf_ref[ts:ts + CONV_HALO, :]

    abuf_ref[CONV_HALO:CONV_HALO + ts, :] = a
    acc = jnp.broadcast_to(cb_ref[...], (ts, A_CH))
    first = CONV_HALO - (CONV_W - 1)
    for r in range(8):
        taps = [w for w in range(CONV_W) if (first + w) % 8 == r]
        span = max((first + w) // 8 for w in taps) * 8 + ts
        shifted_ref = (shift0_ref, shift1_ref)[r % 2]
        shifted_ref[0:span, :] = abuf_ref[r:r + span, :]
        for w in taps:
            base = ((first + w) // 8) * 8
            acc = acc + cw_ref[w:w + 1, :] * shifted_ref[base:base + ts, :]
    a_n = _layer_norm(acc, clg_ref[...], clb_ref[...])
    a_out = a_n * jax.nn.sigmoid(a_n)

    u = _gelu(z[:, 2 * A_CH:2 * A_CH + B_CH])
    v = _layer_norm(_gelu(z[:, 2 * A_CH + B_CH:]), slg_ref[...], slb_ref[...])
    row = lax.broadcasted_iota(jnp.int32, (SGU_CHUNK, SGU_CHUNK), 0)
    col = lax.broadcasted_iota(jnp.int32, (SGU_CHUNK, SGU_CHUNK), 1)
    low_head = lax.broadcasted_iota(jnp.int32, (SGU_CHUNK, LANES), 1) < B_HEAD_DIM
    wm = [jnp.where(col <= row, sw_ref[hh], 0.0).astype(MXU_DTYPE) for hh in range(B_HEADS)]
    for c in range(ts // SGU_CHUNK):
        for pr in range(B_HEADS // 2):
            vp = v[c * SGU_CHUNK:(c + 1) * SGU_CHUNK, pr * LANES:(pr + 1) * LANES].astype(MXU_DTYPE)
            s0 = jnp.dot(wm[2 * pr], vp, preferred_element_type=jnp.float32)
            s1 = jnp.dot(wm[2 * pr + 1], vp, preferred_element_type=jnp.float32)
            s_ref[c * SGU_CHUNK:(c + 1) * SGU_CHUNK, pr * LANES:(pr + 1) * LANES] = (
                jnp.where(low_head, s0, s1) + sb_ref[:, pr * LANES:(pr + 1) * LANES])
    bo = u * s_ref[...]
    out = _mm(a_out, wout_ref[0:A_CH, :]) + _mm(bo, wout_ref[A_CH:, :])
    o_ref[0] = x + out


def _mixer(x, g, w_in, conv_w, conv_b, cln_g, cln_b, sln_g, sln_b, sgu_w, sgu_b, w_out):
    B, S, D = x.shape
    ts = min(MIX_TS, S)
    assert S % ts == 0 and ts % SGU_CHUNK == 0
    row2 = lambda a: a.reshape(1, -1)
    sb_exp = jnp.repeat(sgu_b.T, B_HEAD_DIM, axis=1)
    const = lambda shape: pl.BlockSpec(shape, lambda b, s: (0,) * len(shape))
    return pl.pallas_call(
        _mixer_kernel,
        grid=(B, S // ts),
        in_specs=[
            pl.BlockSpec((1, ts, D), lambda b, s: (b, s, 0)),
            const((1, D)), const((D, MIX_IN)), const((CONV_W, A_CH)), const((1, A_CH)),
            const((1, A_CH)), const((1, A_CH)), const((1, B_CH)), const((1, B_CH)),
            const((B_HEADS, SGU_CHUNK, SGU_CHUNK)), const((SGU_CHUNK, B_CH)), const((A_CH + B_CH, D)),
        ],
        out_specs=pl.BlockSpec((1, ts, D), lambda b, s: (b, s, 0)),
        out_shape=jax.ShapeDtypeStruct((B, S, D), jnp.float32),
        scratch_shapes=[pltpu.VMEM((ts + CONV_HALO, A_CH), jnp.float32),
                        pltpu.VMEM((ts, B_CH), jnp.float32),
                        pltpu.VMEM((ts + CONV_HALO, A_CH), jnp.float32),
                        pltpu.VMEM((ts + CONV_HALO, A_CH), jnp.float32)],
        compiler_params=pltpu.CompilerParams(dimension_semantics=("arbitrary", "arbitrary"),
                                             vmem_limit_bytes=VMEM_LIMIT),
        name="mixer0",
    )(x, row2(g), w_in.astype(MXU_DTYPE), conv_w, row2(conv_b), row2(cln_g), row2(cln_b),
      row2(sln_g), row2(sln_b), sgu_w, sb_exp, w_out.astype(MXU_DTYPE))


def _candidate_pairs():
    return [(a, b) for a in range(PEER_TOPK) for b in range(PEER_TOPK) if (a + 1) * (b + 1) <= PEER_TOPK]


def _top16(s, iota_f, vals_ref, p, hh):
    work = s
    rank = jnp.full(s.shape, float(PEER_TOPK), jnp.float32)
    for r in range(PEER_TOPK):
        m = jnp.max(work, axis=0, keepdims=True)
        first = jnp.min(jnp.where(work == m, iota_f, float(PEER_NKEYS)), axis=0, keepdims=True)
        sel = iota_f == first
        rank = jnp.where(sel, float(r), rank)
        work = jnp.where(sel, -jnp.inf, work)
        vals_ref[p, r, pl.ds(hh, 1), :] = m
    return rank


def _gelu_sigmoid_form(x):
    k0 = -2.0 * math.sqrt(2.0 / math.pi) * math.log2(math.e)
    u = x * (k0 + (k0 * 0.044715) * (x * x))
    return x / (1.0 + jnp.exp2(u))


TOP16_MARK = 2.0 ** 127


def _top16_no_ties(s, vals_ref, p, hh):
    work = s
    for r in range(PEER_TOPK):
        m = jnp.max(work, axis=0, keepdims=True)
        work = jnp.where(work == m, -(1.0 + (PEER_TOPK - 1 - r) / PEER_TOPK) * TOP16_MARK, work)
        vals_ref[p, r, pl.ds(hh, 1), :] = m
    marked = work <= -TOP16_MARK
    rank = jnp.where(marked, (2.0 * PEER_TOPK - 1.0) + work * (PEER_TOPK / TOP16_MARK), float(PEER_TOPK))
    n_marked = jnp.sum(jnp.where(marked, 1.0, 0.0), axis=0, keepdims=True)
    ok = jnp.max(jnp.abs(n_marked - float(PEER_TOPK))) == 0.0
    return rank, ok


def _routed_chunk(a_ref, p_ref, chunk, rank2_ref, e2_ref, n_ref, e1_ref):
    ec, tb = a_ref.shape
    rows_per_tile = 16
    for g in range(ec // PEER_NKEYS):
        i = chunk * (ec // PEER_NKEYS) + g
        n_rows = [n_ref[hh, pl.ds(i, 1), :] for hh in range(PEER_HEADS)]
        e1_rows = [e1_ref[hh, pl.ds(i, 1), :] for hh in range(PEER_HEADS)]
        for lt in range(tb // LANES):
            cols = slice(lt * LANES, (lt + 1) * LANES)
            n_b = [jnp.broadcast_to(n_rows[hh][:, cols], (rows_per_tile, LANES)).astype(ROUTE_DTYPE)
                   for hh in range(PEER_HEADS)]
            e1_b = [jnp.broadcast_to(e1_rows[hh][:, cols], (rows_per_tile, LANES)).astype(ROUTE_DTYPE)
                    for hh in range(PEER_HEADS)]
            for sg in range(PEER_NKEYS // rows_per_tile):
                rows = slice(sg * rows_per_tile, (sg + 1) * rows_per_tile)
                wgt = None
                for hh in range(PEER_HEADS):
                    term = jnp.where(rank2_ref[hh, rows, cols] < n_b[hh], e2_ref[hh, rows, cols],
                                     jnp.zeros((), ROUTE_DTYPE)) * e1_b[hh]
                    wgt = term if wgt is None else wgt + term
                arows = slice(g * PEER_NKEYS + sg * rows_per_tile, g * PEER_NKEYS + (sg + 1) * rows_per_tile)
                act = _gelu_sigmoid_form(a_ref[arows, cols])
                p_ref[arows, cols] = (wgt * act.astype(ROUTE_DTYPE)).astype(p_ref.dtype)


def _peer_kernel(x_ref, g_ref, wq_ref, sk_ref, u_ref, vt_ref, gf_ref, o_ref,
                 ht_ref, yt_ref, q_ref, rank1_ref, rank2_ref, e1_ref, e2_ref, n_ref,
                 vals_ref, na_ref, a0_ref, a1_ref, p0_ref, p1_ref, *, final_norm):
    k = pl.program_id(1)
    tb = x_ref.shape[0]
    ec = a0_ref.shape[0]

    @pl.when(k == 0)
    def _route():
        h = _rms(x_ref[...], g_ref[...])
        ht = h.T.astype(MXU_DTYPE)
        ht_ref[...] = ht
        q_ref[...] = jnp.dot(wq_ref[...], ht, preferred_element_type=jnp.float32).astype(MXU_DTYPE)
        yt_ref[...] = jnp.zeros_like(yt_ref)
        iota_f = lax.broadcasted_iota(jnp.int32, (PEER_NKEYS, tb), 0).astype(jnp.float32)

        def head_scores(hh, carry):
            scores = []
            all_ok = None
            for p in range(2):
                idx = hh * 2 + p
                start = pl.multiple_of(idx * PEER_NKEYS, PEER_NKEYS)
                qhp = q_ref[pl.ds(start, PEER_NKEYS), :]
                s = jnp.dot(sk_ref[idx], qhp.astype(MXU_DTYPE), preferred_element_type=jnp.float32)
                s = s + 0.0
                scores.append(s)
                rank, ok = _top16_no_ties(s, vals_ref, p, hh)
                all_ok = ok if all_ok is None else jnp.logical_and(all_ok, ok)
                e = jnp.exp(s - vals_ref[p, 0, pl.ds(hh, 1), :])
                if p == 0:
                    rank1_ref[hh] = rank
                    e1_ref[hh] = e
                else:
                    rank2_ref[hh] = rank.astype(ROUTE_DTYPE)
                    e2_ref[hh] = e.astype(ROUTE_DTYPE)

            @pl.when(jnp.logical_not(all_ok))
            def _exact():
                rank1_ref[hh] = _top16(scores[0], iota_f, vals_ref, 0, hh)
                rank2_ref[hh] = _top16(scores[1], iota_f, vals_ref, 1, hh).astype(ROUTE_DTYPE)
            return carry

        lax.fori_loop(0, PEER_HEADS, head_scores, 0)

        pairs = _candidate_pairs()
        v1 = [vals_ref[0, a] for a in range(PEER_TOPK)]
        v2 = [vals_ref[1, b] for b in range(PEER_TOPK)]
        cand = {ab: v1[ab[0]] + v2[ab[1]] for ab in pairs}
        pos = {ab: jnp.full((PEER_HEADS, tb), float((ab[0] + 1) * (ab[1] + 1) - 1), jnp.float32) for ab in pairs}
        for i, x_ab in enumerate(pairs):
            for y_ab in pairs[i + 1:]:
                (a, b), (a2, b2) = x_ab, y_ab
                if (a2 >= a and b2 >= b) or (a2 <= a and b2 <= b):
                    continue
                x_first = cand[x_ab] >= cand[y_ab]
                pos[y_ab] = pos[y_ab] + jnp.where(x_first, 1.0, 0.0)
                pos[x_ab] = pos[x_ab] + jnp.where(x_first, 0.0, 1.0)
        c00 = cand[(0, 0)]
        z = jnp.zeros((PEER_HEADS, tb), jnp.float32)
        n_a = [jnp.zeros((PEER_HEADS, tb), jnp.float32) for _ in range(PEER_TOPK)]
        for (a, b) in pairs:
            chosen = pos[(a, b)] < float(PEER_TOPK)
            n_a[a] = n_a[a] + jnp.where(chosen, 1.0, 0.0)
            z = z + jnp.where(chosen, jnp.exp(cand[(a, b)] - c00), 0.0)
        inv_z = 1.0 / z
        half = PEER_TOPK // 2
        for c in range(1, half + 1):
            col_height = jnp.zeros((PEER_HEADS, tb), jnp.float32)
            for a in range(PEER_TOPK // c):
                col_height = col_height + jnp.where(n_a[a] >= float(c), 1.0, 0.0)
            na_ref[c - 1] = col_height
        na_ref[half] = n_a[0]
        na_ref[half + 1] = inv_z

        def head_rows(hh, carry):
            rank1 = rank1_ref[hh]
            n_i = jnp.zeros((PEER_NKEYS, tb), jnp.float32)
            for c in range(1, half + 1):
                n_i = jnp.where(rank1 < na_ref[c - 1, pl.ds(hh, 1), :], float(c), n_i)
            n_i = jnp.where(rank1 == 0.0, na_ref[half, pl.ds(hh, 1), :], n_i)
            n_ref[hh] = n_i
            e1_ref[hh] = e1_ref[hh] * na_ref[half + 1, pl.ds(hh, 1), :]
            return carry

        lax.fori_loop(0, PEER_HEADS, head_rows, 0)

    routing = (rank2_ref, e2_ref, n_ref, e1_ref)
    last = pl.num_programs(1) - 1

    def stage1(a_ref, half):
        a_ref[...] = jnp.dot(u_ref[half * ec:(half + 1) * ec, :], ht_ref[...], preferred_element_type=jnp.float32)

    def stage3(p_ref, half):
        return jnp.dot(vt_ref[:, half * ec:(half + 1) * ec], p_ref[...], preferred_element_type=jnp.float32)

    @pl.when(k == 0)
    def _fill():
        stage1(a0_ref, 0)
        stage1(a1_ref, 1)
        _routed_chunk(a0_ref, p0_ref, 0, *routing)

    @pl.when(jnp.logical_and(k > 0, k < last))
    def _steady():
        y_even = stage3(p0_ref, 0)
        stage1(a0_ref, 0)
        _routed_chunk(a1_ref, p1_ref, 2 * k - 1, *routing)
        y_odd = stage3(p1_ref, 1)
        stage1(a1_ref, 1)
        _routed_chunk(a0_ref, p0_ref, 2 * k, *routing)
        yt_ref[...] += y_even + y_odd

    @pl.when(k == last)
    def _drain():
        y_even = stage3(p0_ref, 0)
        _routed_chunk(a1_ref, p1_ref, 2 * k - 1, *routing)
        yt = yt_ref[...] + y_even + stage3(p1_ref, 1)
        out = x_ref[...] + yt.T
        if final_norm:
            out = _rms(out, gf_ref[...])
        o_ref[...] = out


def _peer(x, g, w_q, subkeys, u_tab, v_tab, g_final, final_norm):
    T, D = x.shape
    tb = min(PEER_TB, T)
    ec = PEER_EC
    assert T % tb == 0 and PEER_N_EXPERTS % (2 * ec) == 0 and ec % PEER_NKEYS == 0
    n_pair = PEER_N_EXPERTS // (2 * ec)
    hq = PEER_HEADS * PEER_DKEY
    wq_t = w_q.T.astype(MXU_DTYPE)
    sk = subkeys.reshape(PEER_HEADS * 2, PEER_NKEYS, PEER_DKEY // 2).astype(MXU_DTYPE)
    u_b = u_tab.astype(MXU_DTYPE)
    vt_b = v_tab.T.astype(MXU_DTYPE)
    head_tile = (PEER_HEADS, PEER_NKEYS, tb)
    return pl.pallas_call(
        functools.partial(_peer_kernel, final_norm=final_norm),
        grid=(T // tb, n_pair + 1),
        in_specs=[
            pl.BlockSpec((tb, D), lambda t, k: (t, 0)),
            pl.BlockSpec((1, D), lambda t, k: (0, 0)),
            pl.BlockSpec((hq, D), lambda t, k: (0, 0), pipeline_mode=pl.Buffered(1)),
            pl.BlockSpec((PEER_HEADS * 2, PEER_NKEYS, PEER_DKEY // 2), lambda t, k: (0, 0, 0)),
            pl.BlockSpec((2 * ec, D), lambda t, k: (jnp.minimum(k, n_pair - 1), 0)),
            pl.BlockSpec((D, 2 * ec), lambda t, k: (0, jnp.maximum(k - 1, 0))),
            pl.BlockSpec((1, D), lambda t, k: (0, 0)),
        ],
        out_specs=pl.BlockSpec((tb, D), lambda t, k: (t, 0)),
        out_shape=jax.ShapeDtypeStruct((T, D), jnp.float32),
        scratch_shapes=[
            pltpu.VMEM((D, tb), MXU_DTYPE),
            pltpu.VMEM((D, tb), jnp.float32),
            pltpu.VMEM((hq, tb), MXU_DTYPE),
            pltpu.VMEM(head_tile, jnp.float32),
            pltpu.VMEM(head_tile, ROUTE_DTYPE),
            pltpu.VMEM(head_tile, jnp.float32),
            pltpu.VMEM(head_tile, ROUTE_DTYPE),
            pltpu.VMEM(head_tile, jnp.float32),
            pltpu.VMEM((2, PEER_TOPK, PEER_HEADS, tb), jnp.float32),
            pltpu.VMEM((PEER_TOPK // 2 + 2, PEER_HEADS, tb), jnp.float32),
            pltpu.VMEM((ec, tb), jnp.float32),
            pltpu.VMEM((ec, tb), jnp.float32),
            pltpu.VMEM((ec, tb), MXU_DTYPE),
            pltpu.VMEM((ec, tb), MXU_DTYPE),
        ],
        compiler_params=pltpu.CompilerParams(dimension_semantics=("arbitrary", "arbitrary"),
                                             vmem_limit_bytes=VMEM_LIMIT),
        name="peer_final" if final_norm else "peer",
    )(x, g.reshape(1, D), wq_t, sk, u_b, vt_b, g_final.reshape(1, D))


def _qkv_kernel(x_ref, g_ref, w_ref, q_ref, k_ref, v_ref):
    h = _rms(x_ref[...], g_ref[...])
    qkv = _mm(h, w_ref[...])
    d = q_ref.shape[1]
    q_ref[...] = qkv[:, :d]
    k_ref[...] = qkv[:, d:2 * d]
    v_ref[...] = qkv[:, 2 * d:]


def _qkv(x, g, w_qkv):
    T, D = x.shape
    tm = min(MM_TM, T)
    aw = MOBA_HEADS * MOBA_HEAD_DIM
    out = jax.ShapeDtypeStruct((T, aw), jnp.float32)
    return pl.pallas_call(
        _qkv_kernel,
        grid=(T // tm,),
        in_specs=[pl.BlockSpec((tm, D), lambda i: (i, 0)),
                  pl.BlockSpec((1, D), lambda i: (0, 0)),
                  pl.BlockSpec((D, 3 * aw), lambda i: (0, 0))],
        out_specs=[pl.BlockSpec((tm, aw), lambda i: (i, 0))] * 3,
        out_shape=[out, out, out],
        compiler_params=pltpu.CompilerParams(dimension_semantics=("arbitrary",),
                                             vmem_limit_bytes=VMEM_LIMIT),
        name="qkv",
    )(x, g.reshape(1, D), w_qkv.astype(MXU_DTYPE))


def _oproj_kernel(x_ref, o_ref, w_ref, y_ref):
    y_ref[...] = x_ref[...] + _mm(o_ref[...], w_ref[...])


def _oproj(x, o, w_o):
    T, D = x.shape
    tm = min(MM_TM, T)
    aw = o.shape[1]
    return pl.pallas_call(
        _oproj_kernel,
        grid=(T // tm,),
        in_specs=[pl.BlockSpec((tm, D), lambda i: (i, 0)),
                  pl.BlockSpec((tm, aw), lambda i: (i, 0)),
                  pl.BlockSpec((aw, D), lambda i: (0, 0))],
        out_specs=pl.BlockSpec((tm, D), lambda i: (i, 0)),
        out_shape=jax.ShapeDtypeStruct((T, D), jnp.float32),
        compiler_params=pltpu.CompilerParams(dimension_semantics=("arbitrary",),
                                             vmem_limit_bytes=VMEM_LIMIT),
        name="oproj",
    )(x, o, w_o.astype(MXU_DTYPE))


def _t5_bucket(dist):
    n = jnp.maximum(dist, 0)
    max_exact = REL_BUCKETS // 2
    nf = jnp.maximum(n, 1).astype(jnp.float32)
    log_b = max_exact + (jnp.log(nf / max_exact) / math.log(REL_MAX_DIST / max_exact)
                         * (REL_BUCKETS - max_exact)).astype(jnp.int32)
    return jnp.where(n < max_exact, n, jnp.minimum(log_b, REL_BUCKETS - 1))


def _bias_kernel(rb_ref, o_ref):
    h = pl.program_id(0)
    dblk = pl.program_id(1)
    key = lax.broadcasted_iota(jnp.int32, (MOBA_BLOCK, MOBA_BLOCK), 0)
    qry = lax.broadcasted_iota(jnp.int32, (MOBA_BLOCK, MOBA_BLOCK), 1)
    dist = dblk * MOBA_BLOCK + qry - key
    bucket = _t5_bucket(dist)
    val = jnp.zeros((MOBA_BLOCK, MOBA_BLOCK), jnp.float32)
    for b in range(REL_BUCKETS):
        val = jnp.where(bucket == b, rb_ref[b, h], val)
    o_ref[0, 0] = jnp.where(dist >= 0, val * LOG2E, NEG)


def _bias_tiles(rel_bias):
    last_tile_min_dist = (BIAS_TILES - 1) * MOBA_BLOCK - (MOBA_BLOCK - 1)
    max_exact = REL_BUCKETS // 2
    first_last_bucket = max_exact * (REL_MAX_DIST / max_exact) ** ((REL_BUCKETS - 1 - max_exact) / (REL_BUCKETS - max_exact))
    assert last_tile_min_dist > first_last_bucket + 1
    return pl.pallas_call(
        _bias_kernel,
        grid=(MOBA_HEADS, BIAS_TILES),
        in_specs=[pl.BlockSpec(memory_space=pltpu.SMEM)],
        out_specs=pl.BlockSpec((1, 1, MOBA_BLOCK, MOBA_BLOCK), lambda h, d: (h, d, 0, 0)),
        out_shape=jax.ShapeDtypeStruct((MOBA_HEADS, BIAS_TILES, MOBA_BLOCK, MOBA_BLOCK), jnp.float32),
        compiler_params=pltpu.CompilerParams(dimension_semantics=("arbitrary", "arbitrary")),
        name="t5_bias_tiles",
    )(rel_bias)


def _attn_kernel(qa_ref, qb_ref, k_ref, v_ref, bias_ref, o_ref,
                 kb_ref, vt_ref, kmean_ref, qs_ref, sel_ref, l_ref, acc_ref,
                 m0_ref, m1_ref, s0_ref, s1_ref):
    j = pl.program_id(2)
    S = k_ref.shape[1]
    n_blk = S // MOBA_BLOCK
    n_pairs = n_blk // 2
    tq = qa_ref.shape[1]

    @pl.when(j == 0)
    def _prep():
        for n in range(n_blk):
            kblk = k_ref[0, n * MOBA_BLOCK:(n + 1) * MOBA_BLOCK, :]
            kb_ref[n * MOBA_BLOCK:(n + 1) * MOBA_BLOCK, :] = kblk.astype(MXU_DTYPE)
            kmean_ref[n:n + 1, :] = jnp.mean(kblk, axis=0, keepdims=True)
            vt_ref[:, n * MOBA_BLOCK:(n + 1) * MOBA_BLOCK] = (
                v_ref[0, n * MOBA_BLOCK:(n + 1) * MOBA_BLOCK, :].T.astype(MXU_DTYPE))

    def unit(u, jj):
        is_a = u <= jj
        t = jnp.where(is_a, 0, 1)
        n = jnp.where(is_a, jj - u, n_blk - u)
        own = jnp.where(is_a, jj, n_blk - 1 - jj)
        return t, n, jnp.minimum(own - n, BIAS_TILES - 1)

    def score(m_ref, s_ref):
        owns = (j, n_blk - 1 - j)
        low_head = lax.broadcasted_iota(jnp.int32, (tq, LANES), 1) < MOBA_HEAD_DIM
        blk = lax.broadcasted_iota(jnp.int32, (n_blk, tq), 0)
        for t, q_ref in enumerate((qa_ref, qb_ref)):
            q = q_ref[0]
            past = blk < owns[t]
            for e in range(2):
                q_e = jnp.where(low_head, q, 0.0) if e == 0 else jnp.where(low_head, 0.0, q)
                gate = jnp.where(past, _mm_nt(kmean_ref[...], q_e), -jnp.inf)
                beaten = jnp.zeros((n_blk, tq), jnp.float32)
                for m in range(n_blk):
                    gm = gate[m:m + 1, :]
                    beats = (gm > gate) | ((gm == gate) & (blk > m))
                    beaten = beaten + jnp.where(beats, 1.0, 0.0)
                chosen = (past & (beaten < float(MOBA_TOPK))) | (blk == owns[t])
                sel_ref[t, e] = jnp.where(chosen, 1.0, 0.0)
                qs_ref[t, e] = (q_e * (MOBA_HEAD_DIM ** -0.5 * LOG2E)).astype(MXU_DTYPE)
                m_ref[t, e] = jnp.full((1, tq), NEG, jnp.float32)
        for u in range(n_blk + 1):
            t, n, tile = unit(u, j)
            k_n = kb_ref[pl.ds(pl.multiple_of(n * MOBA_BLOCK, MOBA_BLOCK), MOBA_BLOCK), :]
            for e in range(2):
                st = _mm_nt(k_n, qs_ref[t, e]) + bias_ref[e, tile]
                st = jnp.where(sel_ref[t, e, pl.ds(n, 1), :] > 0.0, st, NEG)
                s_ref[u, e] = st
                m_ref[t, e] = jnp.maximum(m_ref[t, e], jnp.max(st, axis=0, keepdims=True))

    def finish(m_ref, s_ref):
        jp = j - 1
        for t in range(2):
            for e in range(2):
                l_ref[t, e] = jnp.zeros((1, tq), jnp.float32)
                acc_ref[t, e] = jnp.zeros((LANES, tq), jnp.float32)
        for u in range(n_blk + 1):
            t, n, _ = unit(u, jp)
            vt_n = vt_ref[:, pl.ds(pl.multiple_of(n * MOBA_BLOCK, MOBA_BLOCK), MOBA_BLOCK)]
            for e in range(2):
                p = jnp.exp2(s_ref[u, e] - m_ref[t, e])
                l_ref[t, e] = l_ref[t, e] + jnp.sum(p, axis=0, keepdims=True)
                acc_ref[t, e] = acc_ref[t, e] + jnp.dot(vt_n, p.astype(MXU_DTYPE),
                                                        preferred_element_type=jnp.float32)
        for t, own in enumerate((jp, n_blk - 1 - jp)):
            o0 = acc_ref[t, 0] / l_ref[t, 0]
            o1 = acc_ref[t, 1] / l_ref[t, 1]
            ot = jnp.concatenate([o0[:MOBA_HEAD_DIM], o1[MOBA_HEAD_DIM:]], axis=0)
            o_ref[0, pl.ds(pl.multiple_of(own * tq, tq), tq), :] = ot.T

    bufs = ((m0_ref, s0_ref), (m1_ref, s1_ref))

    @pl.when(j == 0)
    def _first():
        score(*bufs[0])

    for par in range(2):
        @pl.when(jnp.logical_and(jnp.logical_and(j > 0, j < n_pairs), j % 2 == par))
        def _steady(par=par):
            score(*bufs[par])
            finish(*bufs[1 - par])

    @pl.when(j == n_pairs)
    def _last():
        finish(*bufs[(n_pairs - 1) % 2])


def _attention(q, k, v, bias_tiles):
    B, S, aw = q.shape
    tq = ATT_TQ
    assert S % (2 * MOBA_BLOCK) == 0 and tq == MOBA_BLOCK
    n_blk = S // MOBA_BLOCK
    n_pairs = n_blk // 2
    n_pair = aw // LANES
    logits = pltpu.VMEM((n_blk + 1, 2, MOBA_BLOCK, tq), jnp.float32)
    maxima = pltpu.VMEM((2, 2, 1, tq), jnp.float32)
    return pl.pallas_call(
        _attn_kernel,
        grid=(n_pair, B, n_pairs + 1),
        in_specs=[
            pl.BlockSpec((1, tq, LANES), lambda hp, b, j: (b, jnp.minimum(j, n_pairs - 1), hp)),
            pl.BlockSpec((1, tq, LANES), lambda hp, b, j: (b, n_blk - 1 - jnp.minimum(j, n_pairs - 1), hp)),
            pl.BlockSpec((1, S, LANES), lambda hp, b, j: (b, 0, hp)),
            pl.BlockSpec((1, S, LANES), lambda hp, b, j: (b, 0, hp)),
            pl.BlockSpec((2, BIAS_TILES, MOBA_BLOCK, MOBA_BLOCK), lambda hp, b, j: (hp, 0, 0, 0)),
        ],
        out_specs=pl.BlockSpec((1, S, LANES), lambda hp, b, j: (b, 0, hp)),
        out_shape=jax.ShapeDtypeStruct((B, S, aw), jnp.float32),
        scratch_shapes=[
            pltpu.VMEM((S, LANES), MXU_DTYPE),
            pltpu.VMEM((LANES, S), MXU_DTYPE),
            pltpu.VMEM((n_blk, LANES), jnp.float32),
            pltpu.VMEM((2, 2, tq, LANES), MXU_DTYPE),
            pltpu.VMEM((2, 2, n_blk, tq), jnp.float32),
            pltpu.VMEM((2, 2, 1, tq), jnp.float32),
            pltpu.VMEM((2, 2, LANES, tq), jnp.float32),
            maxima, maxima,
            logits, logits,
        ],
        compiler_params=pltpu.CompilerParams(dimension_semantics=("arbitrary", "arbitrary", "arbitrary"),
                                             vmem_limit_bytes=VMEM_LIMIT),
        name="moba_attention",
    )(q, q, k, v, bias_tiles)


def kernel(x, ev_w_in, ev_conv_w, ev_conv_b, ev_conv_ln_g, ev_conv_ln_b, ev_sgu_ln_g, ev_sgu_ln_b, ev_sgu_w, ev_sgu_b, ev_w_out, od_w_qkv, od_w_o, rel_bias, peer_w_q, peer_subkeys, peer_u, peer_v, norm_mix_g, norm_ffn_g, norm_final_g):
    B, S, D = x.shape
    T = B * S
    x = _mixer(x, norm_mix_g[0], ev_w_in[0], ev_conv_w[0], ev_conv_b[0], ev_conv_ln_g[0], ev_conv_ln_b[0],
               ev_sgu_ln_g[0], ev_sgu_ln_b[0], ev_sgu_w[0], ev_sgu_b[0], ev_w_out[0])
    xt = _peer(x.reshape(T, D), norm_ffn_g[0], peer_w_q[0], peer_subkeys[0], peer_u[0], peer_v[0],
               norm_final_g, final_norm=False)
    q, k, v = _qkv(xt, norm_mix_g[1], od_w_qkv[0])
    aw = MOBA_HEADS * MOBA_HEAD_DIM
    o = _attention(q.reshape(B, S, aw), k.reshape(B, S, aw), v.reshape(B, S, aw), _bias_tiles(rel_bias))
    xt = _oproj(xt, o.reshape(T, aw), od_w_o[0])
    xt = _peer(xt, norm_ffn_g[1], peer_w_q[1], peer_subkeys[1], peer_u[1], peer_v[1],
               norm_final_g, final_norm=True)
    return xt.reshape(B, S, D)
```
